```python
import jax, jax.numpy as jnp
from jax import lax
import numpy as np

D_MODEL = 2048
BATCH = 1
SEQ = 16384
DEPTH = 4
DEC_BATCH = 8
DEC_SEQ = 4096
PAST_LEN = 128

HEAD_DIM = 128
GRID_W = 64
A_Q_HEADS = 8
A_KV_HEADS = 2
A_BLOCK = 128
AXIAL_THETA = 10000.0
B_WINDOWS = (128, 512, 2048)
B_DILATIONS = (1, 4, 16)
B_GROUPS = 3
B_HEADS_PER_GROUP = 4
B_HEADS = B_GROUPS * B_HEADS_PER_GROUP
B_BLOCK = 64
ROPE_THETA = 500000.0
ROPE_DIMS = HEAD_DIM // 4
C_HEADS = 8
C_WIN_ROWS = 8
C_WIN_COLS = 16
A_WIDTH = A_Q_HEADS * HEAD_DIM
B_WIDTH = B_HEADS_PER_GROUP * HEAD_DIM
C_WIDTH = C_HEADS * HEAD_DIM
BRANCH_WIDTH = A_WIDTH + B_WIDTH + C_WIDTH
N_BRANCH = 3
PROJ_WIDTH = (A_Q_HEADS + 2 * A_KV_HEADS) * HEAD_DIM + 3 * B_HEADS * HEAD_DIM + 3 * C_HEADS * HEAD_DIM + N_BRANCH * D_MODEL
N_EXPERTS = 16
EXPERT_FF = 2048
EC_CAPACITY = 2
EPS = 1e-6
NEG = -1e30

kernel_name = 'hybrid_bidir_gqa_dilated_natten_ec_moe'


def rms_norm(x, g):
    xf = x.astype(jnp.float32)
    y = xf * lax.rsqrt(jnp.mean(xf * xf, axis=-1, keepdims=True) + EPS)
    return (y * g.astype(jnp.float32)).astype(x.dtype)


def _angles(pos, dims, theta):
    inv = theta ** (-(jnp.arange(0, dims, 2, dtype=jnp.float32) / dims))
    return pos.astype(jnp.float32)[:, None] * inv[None, :]


def _rotate(x, ang):
    xf = x.astype(jnp.float32)
    m = xf.shape[-1] // 2
    c = jnp.cos(ang)[None, :, None, :]
    s = jnp.sin(ang)[None, :, None, :]
    x1, x2 = xf[..., :m], xf[..., m:]
    return jnp.concatenate([x1 * c - x2 * s, x2 * c + x1 * s], axis=-1).astype(x.dtype)


def axial_rope(x):
    t = jnp.arange(x.shape[1])
    half = HEAD_DIM // 2
    row_ang = _angles(t // GRID_W, half, AXIAL_THETA)
    col_ang = _angles(t % GRID_W, half, AXIAL_THETA)
    return jnp.concatenate([_rotate(x[..., :half], row_ang), _rotate(x[..., half:], col_ang)], axis=-1)


def partial_rope(x):
    t = jnp.arange(x.shape[1])
    ang = _angles(t, ROPE_DIMS, ROPE_THETA)
    return jnp.concatenate([_rotate(x[..., :ROPE_DIMS], ang), x[..., ROPE_DIMS:]], axis=-1)


def global_gqa(q, k, v):
    b, L = q.shape[0], q.shape[1]
    grp = A_Q_HEADS // A_KV_HEADS
    nblk = L // A_BLOCK
    qb = q.reshape(b, nblk, A_BLOCK, A_KV_HEADS, grp, HEAD_DIM).transpose(1, 0, 2, 3, 4, 5)
    scale = HEAD_DIM ** -0.5

    def one_block(qi):
        s = jnp.einsum('bqkgd,bskd->bkgqs', qi, k).astype(jnp.float32) * scale
        p = jax.nn.softmax(s, axis=-1).astype(v.dtype)
        return jnp.einsum('bkgqs,bskd->bqkgd', p, v)

    ob = lax.map(one_block, qb)
    return ob.transpose(1, 0, 2, 3, 4, 5).reshape(b, L, A_Q_HEADS * HEAD_DIM)


def band_attention(q, k, v, half):
    b, S, n, H, dh = q.shape
    nb = -(-n // B_BLOCK)
    n_pad = nb * B_BLOCK
    span = B_BLOCK + 2 * half
    qp = jnp.pad(q, ((0, 0), (0, 0), (0, n_pad - n), (0, 0), (0, 0)))
    kp = jnp.pad(k, ((0, 0), (0, 0), (half, n_pad - n + half), (0, 0), (0, 0)))
    vp = jnp.pad(v, ((0, 0), (0, 0), (half, n_pad - n + half), (0, 0), (0, 0)))
    kidx = jnp.arange(nb)[:, None] * B_BLOCK + jnp.arange(span)[None, :]
    kb = kp[:, :, kidx]
    vb = vp[:, :, kidx]
    qb = qp.reshape(b, S, nb, B_BLOCK, H, dh)
    s = jnp.einsum('bsnqhd,bsnkhd->bsnhqk', qb, kb).astype(jnp.float32) * (dh ** -0.5)
    kpos = (kidx - half)[:, None, :]
    qpos = (jnp.arange(nb)[:, None] * B_BLOCK + jnp.arange(B_BLOCK)[None, :])[:, :, None]
    valid = (kpos >= 0) & (kpos < n) & (jnp.abs(kpos - qpos) <= half)
    s = jnp.where(valid[None, None, :, None], s, NEG)
    m = jnp.max(s, axis=-1, keepdims=True)
    p = jnp.exp(s - m)
    l = jnp.sum(p, axis=-1, keepdims=True)
    o = jnp.einsum('bsnhqk,bsnkhd->bsnqhd', (p / l).astype(v.dtype), vb)
    lse = (m + jnp.log(l))[..., 0].transpose(0, 1, 2, 4, 3)
    o = o.reshape(b, S, n_pad, H, dh)[:, :, :n]
    lse = lse.reshape(b, S, n_pad, H)[:, :, :n]
    return o, lse


def dilated_attention(q, k, v, dil, half):
    b, L, H, dh = q.shape
    n = L // dil
    view = lambda x: x.reshape(b, n, dil, H, dh).transpose(0, 2, 1, 3, 4)
    o, lse = band_attention(view(q), view(k), view(v), half)
    o = o.transpose(0, 2, 1, 3, 4).reshape(b, L, H, dh)
    lse = lse.transpose(0, 2, 1, 3).reshape(b, L, H)
    return o, lse


def dilated_mixture(q, k, v):
    b, L = q.shape[0], q.shape[1]
    outs, lses = [], []
    for g in range(B_GROUPS):
        sl = slice(g * B_HEADS_PER_GROUP, (g + 1) * B_HEADS_PER_GROUP)
        half = B_WINDOWS[g] // (2 * B_DILATIONS[g])
        o, lse = dilated_attention(q[:, :, sl], k[:, :, sl], v[:, :, sl], B_DILATIONS[g], half)
        outs.append(o)
        lses.append(lse)
    w = jax.nn.softmax(jnp.stack(lses, axis=0), axis=0)
    o = jnp.sum(w[..., None] * jnp.stack(outs, axis=0).astype(jnp.float32), axis=0).astype(q.dtype)
    return o.reshape(b, L, B_WIDTH)


def neighbourhood_attention(q, k, v, rpb):
    b, L, H, dh = q.shape
    R = L // GRID_W
    kh = min(C_WIN_ROWS, R)
    kw = C_WIN_COLS
    qg = q.reshape(b, R, GRID_W, H, dh)
    kg = k.reshape(b, R, GRID_W, H, dh)
    vg = v.reshape(b, R, GRID_W, H, dh)
    r = jnp.arange(R)
    rs = jnp.clip(r - kh // 2, 0, R - kh)
    rows = rs[:, None] + jnp.arange(kh)[None, :]
    kr = kg[:, rows]
    vr = vg[:, rows]
    s = jnp.einsum('brqhd,brkchd->brhqkc', qg, kr).astype(jnp.float32) * (dh ** -0.5)
    c = jnp.arange(GRID_W)
    cs = jnp.clip(c - kw // 2, 0, GRID_W - kw)
    col_ok = (c[None, :] >= cs[:, None]) & (c[None, :] < cs[:, None] + kw)
    dr = rows - r[:, None]
    dc = jnp.clip(c[None, :] - c[:, None], -(kw - 1), kw - 1)
    bias = rpb[:, (dr + C_WIN_ROWS - 1)[:, None, :, None], (dc + C_WIN_COLS - 1)[None, :, None, :]]
    s = s + bias.transpose(1, 0, 2, 3, 4)[None].astype(jnp.float32)
    s = jnp.where(col_ok[None, None, None, :, None, :], s, NEG)
    p = jax.nn.softmax(s.reshape(b, R, H, GRID_W, kh * GRID_W), axis=-1)
    p = p.reshape(b, R, H, GRID_W, kh, GRID_W).astype(v.dtype)
    o = jnp.einsum('brhqkc,brkchd->brqhd', p, vr)
    return o.reshape(b, L, H * dh)


def expert_choice_ffn(h, w_router, w_gate, w_up, w_down):
    b, L, D = h.shape
    N = b * L
    t = h.reshape(N, D)
    cap = EC_CAPACITY * N // N_EXPERTS
    aff = jax.nn.softmax((t @ w_router).astype(jnp.float32), axis=-1)
    gval, idx = lax.top_k(aff.T, cap)
    xe = t[idx]
    hid = jax.nn.silu(jnp.einsum('ecd,edf->ecf', xe, w_gate)) * jnp.einsum('ecd,edf->ecf', xe, w_up)
    ye = jnp.einsum('ecf,efd->ecd', hid, w_down) * gval[..., None].astype(h.dtype)
    out = jnp.zeros((N, D), h.dtype).at[idx.reshape(-1)].add(ye.reshape(-1, D))
    return out.reshape(b, L, D)


def trunk_layer(x, norm_attn, w_in, qk_gain, rpb, w_branch, w_out, norm_ffn, w_router, w_gate, w_up, w_down):
    b, L, D = x.shape
    h = rms_norm(x, norm_attn)
    proj = h @ w_in
    sizes = (A_Q_HEADS * HEAD_DIM, A_KV_HEADS * HEAD_DIM, A_KV_HEADS * HEAD_DIM,
             B_HEADS * HEAD_DIM, B_HEADS * HEAD_DIM, B_HEADS * HEAD_DIM,
             C_HEADS * HEAD_DIM, C_HEADS * HEAD_DIM, C_HEADS * HEAD_DIM,
             D_MODEL, D_MODEL, D_MODEL)
    splits = [int(s) for s in np.cumsum(sizes)[:-1]]
    qa, ka, va, qb, kb, vb, qc, kc, vc, ga, gb, gc = jnp.split(proj, splits, axis=-1)
    heads = lambda z, n: z.reshape(b, L, n, HEAD_DIM)
    qa = axial_rope(rms_norm(heads(qa, A_Q_HEADS), qk_gain[0, 0]))
    ka = axial_rope(rms_norm(heads(ka, A_KV_HEADS), qk_gain[0, 1]))
    oa = global_gqa(qa, ka, heads(va, A_KV_HEADS))
    qb = partial_rope(rms_norm(heads(qb, B_HEADS), qk_gain[1, 0]))
    kb = partial_rope(rms_norm(heads(kb, B_HEADS), qk_gain[1, 1]))
    ob = dilated_mixture(qb, kb, heads(vb, B_HEADS))
    qc = rms_norm(heads(qc, C_HEADS), qk_gain[2, 0])
    kc = rms_norm(heads(kc, C_HEADS), qk_gain[2, 1])
    oc = neighbourhood_attention(qc, kc, heads(vc, C_HEADS), rpb)
    ya = oa @ w_branch[:A_WIDTH]
    yb = ob @ w_branch[A_WIDTH:A_WIDTH + B_WIDTH]
    yc = oc @ w_branch[A_WIDTH + B_WIDTH:]
    merged = jax.nn.sigmoid(ga) * ya + jax.nn.sigmoid(gb) * yb + jax.nn.sigmoid(gc) * yc
    x = x + merged @ w_out
    x = x + expert_choice_ffn(rms_norm(x, norm_ffn), w_router, w_gate, w_up, w_down)
    return x


def setup_inputs(seed: int = 0) -> dict:
    key = jax.random.key(seed)
    ks = jax.random.split(key, 16)
    f32 = jnp.float32
    nrm = lambda k, shape, scale: jax.random.normal(k, shape, f32) * scale
    w_branch = jnp.concatenate([
        nrm(ks[6], (DEPTH, A_WIDTH, D_MODEL), A_WIDTH ** -0.5),
        nrm(ks[7], (DEPTH, B_WIDTH, D_MODEL), B_WIDTH ** -0.5),
        nrm(ks[8], (DEPTH, C_WIDTH, D_MODEL), C_WIDTH ** -0.5)], axis=1)
    return {
        'x_prompt': nrm(ks[0], (BATCH, SEQ, D_MODEL), 1.0),
        'x_sample': nrm(ks[1], (DEC_BATCH, DEC_SEQ, D_MODEL), 1.0),
        'norm_attn': 1.0 + nrm(ks[2], (DEPTH, D_MODEL), 0.02),
        'w_in': nrm(ks[3], (DEPTH, D_MODEL, PROJ_WIDTH), D_MODEL ** -0.5),
        'qk_gain': 1.0 + nrm(ks[4], (DEPTH, 3, 2, HEAD_DIM), 0.02),
        'rpb': nrm(ks[5], (DEPTH, C_HEADS, 2 * C_WIN_ROWS - 1, 2 * C_WIN_COLS - 1), 0.1),
        'w_branch': w_branch,
        'w_out': nrm(ks[9], (DEPTH, D_MODEL, D_MODEL), D_MODEL ** -0.5),
        'norm_ffn': 1.0 + nrm(ks[10], (DEPTH, D_MODEL), 0.02),
        'w_router': nrm(ks[11], (DEPTH, D_MODEL, N_EXPERTS), D_MODEL ** -0.5),
        'w_gate': nrm(ks[12], (DEPTH, N_EXPERTS, D_MODEL, EXPERT_FF), D_MODEL ** -0.5),
        'w_up': nrm(ks[13], (DEPTH, N_EXPERTS, D_MODEL, EXPERT_FF), D_MODEL ** -0.5),
        'w_down': nrm(ks[14], (DEPTH, N_EXPERTS, EXPERT_FF, D_MODEL), EXPERT_FF ** -0.5),
    }


def reference(x_prompt, x_sample, norm_attn, w_in, qk_gain, rpb, w_branch, w_out, norm_ffn, w_router, w_gate, w_up, w_down):
    y_prompt = x_prompt
    y_sample = x_sample
    for i in range(DEPTH):
        args = (norm_attn[i], w_in[i], qk_gain[i], rpb[i], w_branch[i], w_out[i], norm_ffn[i],
                w_router[i], w_gate[i], w_up[i], w_down[i])
        y_prompt = trunk_layer(y_prompt, *args)
        y_sample = trunk_layer(y_sample, *args)
    return (y_prompt, y_sample)
```

```python
import functools

import jax
import jax.numpy as jnp
from jax import lax
from jax.experimental import pallas as pl
from jax.experimental.pallas import tpu as pltpu

F32 = jnp.float32
BF16 = jnp.bfloat16

D_MODEL = 2048
HEAD_DIM = 128
GRID_W = 64
A_Q_HEADS = 8
A_KV_HEADS = 2
AXIAL_THETA = 10000.0
B_WINDOWS = (128, 512, 2048)
B_DILATIONS = (1, 4, 16)
B_GROUPS = 3
B_HEADS_PER_GROUP = 4
B_HEADS = B_GROUPS * B_HEADS_PER_GROUP
B_HALF = 64
ROPE_THETA = 500000.0
ROPE_DIMS = HEAD_DIM // 4
C_HEADS = 8
C_WIN_ROWS = 8
C_WIN_COLS = 16
A_WIDTH = A_Q_HEADS * HEAD_DIM
B_WIDTH = B_HEADS_PER_GROUP * HEAD_DIM
C_WIDTH = C_HEADS * HEAD_DIM
QKV_WIDTH = (A_Q_HEADS + 2 * A_KV_HEADS + 3 * B_HEADS + 3 * C_HEADS) * HEAD_DIM
PROJ_WIDTH = QKV_WIDTH + 3 * D_MODEL
N_EXPERTS = 16
EXPERT_FF = 2048
EC_CAPACITY = 2
EPS = 1e-6
NEG = -1e30
SCALE = HEAD_DIM ** -0.5

QA_H, KA_H, VA_H = 0, 8, 10
QB_H, KB_H, VB_H = 12, 24, 36
QC_H, KC_H, VC_H = 48, 56, 64

VMEM_LIMIT = 56 * 1024 * 1024

PLAIN, NORM, AXIAL, PARTIAL = 0, 1, 2, 3


def _params(*sem):
    return pltpu.CompilerParams(dimension_semantics=sem, vmem_limit_bytes=VMEM_LIMIT)


def _rms_rows(xf, g):
    ms = jnp.mean(xf * xf, axis=-1, keepdims=True)
    return xf * lax.rsqrt(ms + EPS) * g


def _angles(pos, dims, theta):
    inv = theta ** (-(jnp.arange(0, dims, 2, dtype=F32) / dims))
    return pos.astype(F32)[:, None] * inv[None, :]


def _axial_tables(L):
    t = jnp.arange(L)
    ra = _angles(t // GRID_W, HEAD_DIM // 2, AXIAL_THETA)
    ca = _angles(t % GRID_W, HEAD_DIM // 2, AXIAL_THETA)
    cos = jnp.concatenate([jnp.cos(ra), jnp.cos(ra), jnp.cos(ca), jnp.cos(ca)], axis=-1)
    sin = jnp.concatenate([-jnp.sin(ra), jnp.sin(ra), -jnp.sin(ca), jnp.sin(ca)], axis=-1)
    return cos, sin


def _partial_tables(L):
    t = jnp.arange(L)
    ang = _angles(t, ROPE_DIMS, ROPE_THETA)
    rest = HEAD_DIM - ROPE_DIMS
    cos = jnp.concatenate([jnp.cos(ang), jnp.cos(ang), jnp.ones((L, rest), F32)], axis=-1)
    sin = jnp.concatenate([-jnp.sin(ang), jnp.sin(ang), jnp.zeros((L, rest), F32)], axis=-1)
    return cos, sin


PROJ_TN = 512
_PROJ_PATTERNS = (
    (lambda j: j < 2, (AXIAL,) * 4),
    (lambda j: j == 2, (AXIAL, AXIAL, PLAIN, PLAIN)),
    (lambda j: (j >= 3) & (j < 9), (PARTIAL,) * 4),
    (lambda j: (j >= 12) & (j < 16), (NORM,) * 4),
    (lambda j: ((j >= 9) & (j < 12)) | (j >= 16), (PLAIN,) * 4),
)


def _proj_kernel(x_ref, g_ref, w_ref, gain_ref, cax_ref, sax_ref, cpr_ref, spr_ref,
                 o_ref, h_ref, acc_ref, *, tm, rc):
    j = pl.program_id(1)

    @pl.when(j == 0)
    def _():
        h_ref[...] = _rms_rows(x_ref[...], g_ref[...]).astype(BF16)

    acc_ref[...] = jnp.dot(h_ref[...], w_ref[...], preferred_element_type=F32)

    lane = lax.broadcasted_iota(jnp.int32, (rc, HEAD_DIM), 1)
    ax_first = (lane % 64) < 32
    pr_first = lane < 16

    def epilogue(kinds):
        def body(c, carry):
            r0 = pl.multiple_of(c * rc, rc)
            rows = pl.ds(r0, rc)
            for hd, kind in enumerate(kinds):
                cols = slice(hd * HEAD_DIM, (hd + 1) * HEAD_DIM)
                y = acc_ref[rows, cols]
                if kind != PLAIN:
                    y = _rms_rows(y, gain_ref[:, cols])
                if kind == AXIAL:
                    sw = jnp.where(ax_first, pltpu.roll(y, 96, 1), pltpu.roll(y, 32, 1))
                    y = y * cax_ref[rows, :] + sw * sax_ref[rows, :]
                elif kind == PARTIAL:
                    sw = jnp.where(pr_first, pltpu.roll(y, 112, 1), pltpu.roll(y, 16, 1))
                    y = y * cpr_ref[rows, :] + sw * spr_ref[rows, :]
                o_ref[rows, cols] = y.astype(BF16)
            return carry
        lax.fori_loop(0, tm // rc, body, 0)

    for cond, kinds in _PROJ_PATTERNS:
        pl.when(cond(j))(functools.partial(epilogue, kinds))


def _proj_qkv(x, g, w_in, layer, gain, tabs, L):
    N = x.shape[0]
    tm = min(1024, L)
    nl = L // tm
    rc = 256
    cax, sax, cpr, spr = tabs
    tab_spec = pl.BlockSpec((tm, HEAD_DIM), lambda i, j: (i % nl, 0))
    return pl.pallas_call(
        functools.partial(_proj_kernel, tm=tm, rc=rc),
        grid=(N // tm, QKV_WIDTH // PROJ_TN),
        in_specs=[
            pl.BlockSpec((tm, D_MODEL), lambda i, j: (i, 0)),
            pl.BlockSpec((1, D_MODEL), lambda i, j: (0, 0)),
            pl.BlockSpec((None, D_MODEL, PROJ_TN), lambda i, j: (layer, 0, j)),
            pl.BlockSpec((1, PROJ_TN), lambda i, j: (0, j)),
            tab_spec, tab_spec, tab_spec, tab_spec,
        ],
        out_specs=pl.BlockSpec((tm, PROJ_TN), lambda i, j: (i, j)),
        out_shape=jax.ShapeDtypeStruct((N, QKV_WIDTH), BF16),
        scratch_shapes=[pltpu.VMEM((tm, D_MODEL), BF16), pltpu.VMEM((tm, PROJ_TN), F32)],
        compiler_params=_params("parallel", "arbitrary"),
        name="proj_qkv",
    )(x, g, w_in, gain, cax, sax, cpr, spr)


def _attn_a_kernel(q_ref, k_ref, v_ref, o_ref, qs_ref, m_ref, l_ref, acc_ref, *, tq, tk, L):
    grp = A_Q_HEADS // A_KV_HEADS
    for h in range(grp):
        qs_ref[h * tq:(h + 1) * tq, :] = q_ref[:, h * HEAD_DIM:(h + 1) * HEAD_DIM]
    m_ref[...] = jnp.full(m_ref.shape, -jnp.inf, F32)
    l_ref[...] = jnp.zeros(l_ref.shape, F32)
    acc_ref[...] = jnp.zeros(acc_ref.shape, F32)

    def body(c, carry):
        c0 = pl.multiple_of(c * tk, tk)
        k = k_ref[pl.ds(c0, tk), :]
        v = v_ref[pl.ds(c0, tk), :]
        s = lax.dot_general(qs_ref[...], k, (((1,), (1,)), ((), ())),
                            preferred_element_type=F32) * SCALE
        m_prev = m_ref[...]
        m_next = jnp.maximum(m_prev, jnp.max(s, axis=1, keepdims=True))
        p = jnp.exp(s - pltpu.repeat(m_next, tk // HEAD_DIM, axis=1))
        alpha = jnp.exp(m_prev - m_next)
        l_ref[...] = alpha * l_ref[...] + jnp.sum(p, axis=1, keepdims=True)
        acc_ref[...] = acc_ref[...] * alpha + jnp.dot(p.astype(BF16), v,
                                                      preferred_element_type=F32)
        m_ref[...] = m_next
        return carry

    lax.fori_loop(0, L // tk, body, 0)
    o = acc_ref[...] / l_ref[...]
    for h in range(grp):
        o_ref[:, h * HEAD_DIM:(h + 1) * HEAD_DIM] = o[h * tq:(h + 1) * tq].astype(BF16)


def _attn_a(qkv3):
    b, L, _ = qkv3.shape
    grp = A_Q_HEADS // A_KV_HEADS
    tq = 256
    tk = 512
    rows = grp * tq
    return pl.pallas_call(
        functools.partial(_attn_a_kernel, tq=tq, tk=tk, L=L),
        grid=(b, A_KV_HEADS, L // tq),
        in_specs=[
            pl.BlockSpec((None, tq, grp * HEAD_DIM), lambda bi, kv, qi: (bi, qi, kv)),
            pl.BlockSpec((None, L, HEAD_DIM), lambda bi, kv, qi: (bi, 0, KA_H + kv)),
            pl.BlockSpec((None, L, HEAD_DIM), lambda bi, kv, qi: (bi, 0, VA_H + kv)),
        ],
        out_specs=pl.BlockSpec((None, tq, grp * HEAD_DIM), lambda bi, kv, qi: (bi, qi, kv)),
        out_shape=jax.ShapeDtypeStruct((b, L, A_WIDTH), BF16),
        scratch_shapes=[
            pltpu.VMEM((rows, HEAD_DIM), BF16),
            pltpu.VMEM((rows, HEAD_DIM), F32),
            pltpu.VMEM((rows, HEAD_DIM), F32),
            pltpu.VMEM((rows, HEAD_DIM), F32),
        ],
        compiler_params=_params("parallel", "parallel", "arbitrary"),
        name="attn_a",
    )(qkv3, qkv3, qkv3)


B_SUB = 128


def _attn_b_kernel(q_ref, kc_ref, kp_ref, kn_ref, vc_ref, vp_ref, vn_ref, o_ref, lse_ref,
                   *, tq, n):
    i = pl.program_id(2)
    span = B_SUB + 2 * B_HALF
    row = lax.broadcasted_iota(jnp.int32, (B_SUB, span), 0)
    col = lax.broadcasted_iota(jnp.int32, (B_SUB, span), 1)
    band = (col >= row) & (col - row <= 2 * B_HALF)
    for h in range(B_HEADS_PER_GROUP):
        cols = slice(h * HEAD_DIM, (h + 1) * HEAD_DIM)
        kcat = jnp.concatenate([kp_ref[:, cols], kc_ref[:, cols], kn_ref[:, cols]], axis=0)
        vcat = jnp.concatenate([vp_ref[:, cols], vc_ref[:, cols], vn_ref[:, cols]], axis=0)
        for jq in range(tq // B_SUB):
            rows = slice(jq * B_SUB, (jq + 1) * B_SUB)
            q = q_ref[rows, cols]
            ks = kcat[jq * B_SUB:jq * B_SUB + span]
            vs = vcat[jq * B_SUB:jq * B_SUB + span]
            s = lax.dot_general(q, ks, (((1,), (1,)), ((), ())),
                                preferred_element_type=F32) * SCALE
            kpos = col + (i * tq + jq * B_SUB - B_HALF)
            valid = band & (kpos >= 0) & (kpos < n)
            s = jnp.where(valid, s, NEG)
            m = jnp.max(s, axis=1, keepdims=True)
            p = jnp.exp(s - m)
            l = jnp.sum(p, axis=1, keepdims=True)
            pn = (p / l).astype(BF16)
            o_ref[rows, cols] = jnp.dot(pn, vs, preferred_element_type=F32)
            lse_ref[rows, cols] = jnp.broadcast_to(m + jnp.log(l), (B_SUB, HEAD_DIM))


def _attn_b_group(qkv, b, L, g):
    dil = B_DILATIONS[g]
    n = L // dil
    tq = min(512, n)
    hb = tq // B_HALF
    nhb = n // B_HALF
    wb = QKV_WIDTH // B_WIDTH
    view = qkv.reshape(b, n, dil * QKV_WIDTH)
    qcol = QB_H // B_HEADS_PER_GROUP + g
    kcol = KB_H // B_HEADS_PER_GROUP + g
    vcol = VB_H // B_HEADS_PER_GROUP + g

    def cur(cb):
        return pl.BlockSpec((None, tq, B_WIDTH), lambda bi, r, i: (bi, i, r * wb + cb))

    def prev(cb):
        return pl.BlockSpec((None, B_HALF, B_WIDTH),
                            lambda bi, r, i: (bi, jnp.maximum(i * hb - 1, 0), r * wb + cb))

    def nxt(cb):
        return pl.BlockSpec((None, B_HALF, B_WIDTH),
                            lambda bi, r, i: (bi, jnp.minimum((i + 1) * hb, nhb - 1), r * wb + cb))

    out_spec = pl.BlockSpec((None, tq, B_WIDTH), lambda bi, r, i: (bi, i, r))
    out_sds = jax.ShapeDtypeStruct((b, n, dil * B_WIDTH), F32)
    o, lse = pl.pallas_call(
        functools.partial(_attn_b_kernel, tq=tq, n=n),
        grid=(b, dil, n // tq),
        in_specs=[cur(qcol), cur(kcol), prev(kcol), nxt(kcol), cur(vcol), prev(vcol), nxt(vcol)],
        out_specs=(out_spec, out_spec),
        out_shape=(out_sds, out_sds),
        compiler_params=_params("parallel", "parallel", "arbitrary"),
        name=f"attn_b{g}",
    )(view, view, view, view, view, view, view)
    return o.reshape(b * L, B_WIDTH), lse.reshape(b * L, B_WIDTH)


def _b_combine_kernel(o0, o1, o2, l0, l1, l2, out_ref):
    a0, a1, a2 = l0[...], l1[...], l2[...]
    m = jnp.maximum(jnp.maximum(a0, a1), a2)
    e0, e1, e2 = jnp.exp(a0 - m), jnp.exp(a1 - m), jnp.exp(a2 - m)
    den = e0 + e1 + e2
    out = (e0 / den) * o0[...] + (e1 / den) * o1[...] + (e2 / den) * o2[...]
    out_ref[...] = out.astype(BF16)


def _b_combine(os, lses):
    N = os[0].shape[0]
    tm = 1024
    spec = pl.BlockSpec((tm, B_WIDTH), lambda i: (i, 0))
    return pl.pallas_call(
        _b_combine_kernel,
        grid=(N // tm,),
        in_specs=[spec] * 6,
        out_specs=spec,
        out_shape=jax.ShapeDtypeStruct((N, B_WIDTH), BF16),
        compiler_params=_params("parallel"),
        name="b_combine",
    )(*os, *lses)


def _c_bias_table(rpb):
    c = jnp.arange(GRID_W)
    cs = jnp.clip(c - C_WIN_COLS // 2, 0, GRID_W - C_WIN_COLS)
    col_ok = (c[None, :] >= cs[:, None]) & (c[None, :] < cs[:, None] + C_WIN_COLS)
    dc = jnp.clip(c[None, :] - c[:, None], -(C_WIN_COLS - 1), C_WIN_COLS - 1) + C_WIN_COLS - 1
    cb = rpb[:, :, dc].astype(F32)
    cb = jnp.where(col_ok[None, None], cb, NEG)
    a = jnp.arange(C_WIN_ROWS)[:, None] + jnp.arange(C_WIN_ROWS)[None, :]
    bt = cb[:, a]
    bt = bt.transpose(0, 1, 3, 2, 4)
    return bt.reshape(C_HEADS, C_WIN_ROWS, GRID_W, C_WIN_ROWS * GRID_W)


def _attn_c_kernel(q_ref, k_ref, v_ref, bt_ref, o_ref, *, rows_per_step, R):
    qi = pl.program_id(2)
    kw = C_WIN_ROWS * GRID_W

    def body(rr, carry):
        r = qi * rows_per_step + rr
        rs = jnp.clip(r - C_WIN_ROWS // 2, 0, R - C_WIN_ROWS)
        delta = rs - r + (C_WIN_ROWS - 1)
        q0 = pl.multiple_of(rr * GRID_W, GRID_W)
        k0 = pl.multiple_of(rs * GRID_W, GRID_W)
        q = q_ref[pl.ds(q0, GRID_W), :]
        k = k_ref[pl.ds(k0, kw), :]
        v = v_ref[pl.ds(k0, kw), :]
        s = lax.dot_general(q, k, (((1,), (1,)), ((), ())),
                            preferred_element_type=F32) * SCALE + bt_ref[delta]
        m = jnp.max(s, axis=1, keepdims=True)
        p = jnp.exp(s - m)
        l = jnp.sum(p, axis=1, keepdims=True)
        pn = (p / l).astype(BF16)
        o_ref[pl.ds(q0, GRID_W), :] = jnp.dot(pn, v, preferred_element_type=F32).astype(BF16)
        return carry

    lax.fori_loop(0, rows_per_step, body, 0)


def _attn_c(qkv3, bt):
    b, L, _ = qkv3.shape
    R = L // GRID_W
    assert R >= C_WIN_ROWS
    tq = min(2048, L)
    return pl.pallas_call(
        functools.partial(_attn_c_kernel, rows_per_step=tq // GRID_W, R=R),
        grid=(b, C_HEADS, L // tq),
        in_specs=[
            pl.BlockSpec((None, tq, HEAD_DIM), lambda bi, h, qi: (bi, qi, QC_H + h)),
            pl.BlockSpec((None, L, HEAD_DIM), lambda bi, h, qi: (bi, 0, KC_H + h)),
            pl.BlockSpec((None, L, HEAD_DIM), lambda bi, h, qi: (bi, 0, VC_H + h)),
            pl.BlockSpec((None, C_WIN_ROWS, GRID_W, C_WIN_ROWS * GRID_W),
                         lambda bi, h, qi: (h, 0, 0, 0)),
        ],
        out_specs=pl.BlockSpec((None, tq, HEAD_DIM), lambda bi, h, qi: (bi, qi, h)),
        out_shape=jax.ShapeDtypeStruct((b, L, C_WIDTH), BF16),
        compiler_params=_params("parallel", "parallel", "arbitrary"),
        name="attn_c",
    )(qkv3, qkv3, qkv3, bt)


MERGE_TN = 512


def _merge_kernel(x_ref, g_ref, oa_ref, ob_ref, oc_ref, wga_ref, wgb_ref, wgc_ref,
                  wba_ref, wbb_ref, wbc_ref, o_ref, h_ref):
    @pl.when(pl.program_id(1) == 0)
    def _():
        h_ref[...] = _rms_rows(x_ref[...], g_ref[...]).astype(BF16)

    h = h_ref[...]

    def branch(wg_ref, o_in_ref, wb_ref):
        gate = jax.nn.sigmoid(jnp.dot(h, wg_ref[...], preferred_element_type=F32))
        return gate * jnp.dot(o_in_ref[...], wb_ref[...], preferred_element_type=F32)

    merged = branch(wga_ref, oa_ref, wba_ref)
    merged = merged + branch(wgb_ref, ob_ref, wbb_ref)
    merged = merged + branch(wgc_ref, oc_ref, wbc_ref)
    o_ref[...] = merged.astype(BF16)


def _merge(x, g, oa, ob, oc, w_in, wba, wbb, wbc, layer):
    N = x.shape[0]
    tm = 512
    tn = MERGE_TN
    gcol = QKV_WIDTH // tn
    dcol = D_MODEL // tn

    def rows(width):
        return pl.BlockSpec((tm, width), lambda i, j: (i, 0))

    def gate_w(k):
        return pl.BlockSpec((None, D_MODEL, tn), lambda i, j: (layer, 0, gcol + k * dcol + j))

    def branch_w(width):
        return pl.BlockSpec((None, width, tn), lambda i, j: (layer, 0, j))

    return pl.pallas_call(
        _merge_kernel,
        grid=(N // tm, dcol),
        in_specs=[
            rows(D_MODEL),
            pl.BlockSpec((1, D_MODEL), lambda i, j: (0, 0)),
            rows(A_WIDTH), rows(B_WIDTH), rows(C_WIDTH),
            gate_w(0), gate_w(1), gate_w(2),
            branch_w(A_WIDTH), branch_w(B_WIDTH), branch_w(C_WIDTH),
        ],
        out_specs=pl.BlockSpec((tm, tn), lambda i, j: (i, j)),
        out_shape=jax.ShapeDtypeStruct((N, D_MODEL), BF16),
        scratch_shapes=[pltpu.VMEM((tm, D_MODEL), BF16)],
        compiler_params=_params("parallel", "arbitrary"),
        name="merge",
    )(x, g, oa, ob, oc, w_in, w_in, w_in, wba, wbb, wbc)


def _out_kernel(m_ref, x_ref, w_ref, g_ref, wr_ref, xo_ref, h_ref, aff_ref):
    xn = x_ref[...] + jnp.dot(m_ref[...], w_ref[...], preferred_element_type=F32)
    xo_ref[...] = xn
    h = _rms_rows(xn, g_ref[...]).astype(BF16)
    h_ref[...] = h
    logits = lax.dot_general(wr_ref[...], h, (((1,), (1,)), ((), ())),
                             preferred_element_type=F32)
    mx = jnp.max(logits, axis=0, keepdims=True)
    e = jnp.exp(logits - mx)
    aff_ref[...] = e / jnp.sum(e, axis=0, keepdims=True)


def _out_proj(merged, x, w_out, g, w_rt, layer):
    N = x.shape[0]
    tm = 512
    row = pl.BlockSpec((tm, D_MODEL), lambda i: (i, 0))
    return pl.pallas_call(
        _out_kernel,
        grid=(N // tm,),
        in_specs=[
            row, row,
            pl.BlockSpec((None, D_MODEL, D_MODEL), lambda i: (layer, 0, 0)),
            pl.BlockSpec((1, D_MODEL), lambda i: (0, 0)),
            pl.BlockSpec((None, N_EXPERTS, D_MODEL), lambda i: (layer, 0, 0)),
        ],
        out_specs=(row, row, pl.BlockSpec((N_EXPERTS, tm), lambda i: (0, i))),
        out_shape=(
            jax.ShapeDtypeStruct((N, D_MODEL), F32),
            jax.ShapeDtypeStruct((N, D_MODEL), BF16),
            jax.ShapeDtypeStruct((N_EXPERTS, N), F32),
        ),
        compiler_params=_params("parallel"),
        name="out_proj",
    )(merged, x, w_out, g, w_rt)


def _ffn_kernel(xe_ref, wg_ref, wu_ref, wd_ref, gv_ref, o_ref):
    f = pl.program_id(2)
    xe = xe_ref[...]
    gate = jnp.dot(xe, wg_ref[...].astype(BF16), preferred_element_type=F32)
    up = jnp.dot(xe, wu_ref[...].astype(BF16), preferred_element_type=F32)
    hid = (jax.nn.silu(gate) * up).astype(BF16)
    part = jnp.dot(hid, wd_ref[...].astype(BF16), preferred_element_type=F32)

    @pl.when(f == 0)
    def _():
        o_ref[...] = part

    @pl.when(f > 0)
    def _():
        o_ref[...] += part

    @pl.when(f == pl.num_programs(2) - 1)
    def _():
        o_ref[...] = o_ref[...] * gv_ref[...]


def _ffn(xe, w_gate, w_up, w_down, gval, layer):
    E, cap, _ = xe.shape
    ct = min(1024, cap)
    tf = 256
    nc = cap // ct
    return pl.pallas_call(
        _ffn_kernel,
        grid=(E, nc, EXPERT_FF // tf),
        in_specs=[
            pl.BlockSpec((None, ct, D_MODEL), lambda e, c, f: (e, c, 0)),
            pl.BlockSpec((None, None, D_MODEL, tf), lambda e, c, f: (layer, e, 0, f)),
            pl.BlockSpec((None, None, D_MODEL, tf), lambda e, c, f: (layer, e, 0, f)),
            pl.BlockSpec((None, None, tf, D_MODEL), lambda e, c, f: (layer, e, f, 0)),
            pl.BlockSpec((ct, 1), lambda e, c, f: (e * nc + c, 0)),
        ],
        out_specs=pl.BlockSpec((None, ct, D_MODEL), lambda e, c, f: (e, c, 0)),
        out_shape=jax.ShapeDtypeStruct((E, cap, D_MODEL), F32),
        compiler_params=_params("parallel", "parallel", "arbitrary"),
        name="expert_ffn",
    )(xe, w_gate, w_up, w_down, gval)


def _layer(x, b, L, layer, W, tabs):
    N = b * L
    qkv = _proj_qkv(x, W["norm_attn"][layer][None], W["w_in"], layer, W["gain"][layer], tabs, L)
    qkv3 = qkv.reshape(b, L, QKV_WIDTH)
    oa = _attn_a(qkv3).reshape(N, A_WIDTH)
    ob_parts = [_attn_b_group(qkv, b, L, g) for g in range(B_GROUPS)]
    ob = _b_combine([p[0] for p in ob_parts], [p[1] for p in ob_parts])
    oc = _attn_c(qkv3, W["c_bias"][layer]).reshape(N, C_WIDTH)
    merged = _merge(x, W["norm_attn"][layer][None], oa, ob, oc, W["w_in"],
                    W["wb_a"], W["wb_b"], W["wb_c"], layer)
    x, h2, aff_t = _out_proj(merged, x, W["w_out"], W["norm_ffn"][layer][None], W["w_rt"], layer)
    cap = EC_CAPACITY * N // N_EXPERTS
    gval, idx = lax.top_k(aff_t, cap)
    flat = idx.reshape(-1)
    xe = jnp.take(h2, flat, axis=0).reshape(N_EXPERTS, cap, D_MODEL)
    ye = _ffn(xe, W["w_gate"], W["w_up"], W["w_down"], gval.reshape(-1, 1), layer)
    return x.at[flat].add(ye.reshape(-1, D_MODEL))


def _trunk(x3, W, depth):
    b, L, _ = x3.shape
    tabs = _axial_tables(L) + _partial_tables(L)
    x = x3.reshape(b * L, D_MODEL)
    for layer in range(depth):
        x = _layer(x, b, L, layer, W, tabs)
    return x.reshape(b, L, D_MODEL)


def _prepare(norm_attn, w_in, qk_gain, rpb, w_branch, w_out, norm_ffn, w_router, w_gate, w_up,
             w_down):
    depth = w_in.shape[0]
    ones = jnp.ones((depth, HEAD_DIM), F32)

    def rep(v, k):
        return jnp.tile(v, (1, k))

    gain = jnp.concatenate([
        rep(qk_gain[:, 0, 0], A_Q_HEADS), rep(qk_gain[:, 0, 1], A_KV_HEADS), rep(ones, A_KV_HEADS),
        rep(qk_gain[:, 1, 0], B_HEADS), rep(qk_gain[:, 1, 1], B_HEADS), rep(ones, B_HEADS),
        rep(qk_gain[:, 2, 0], C_HEADS), rep(qk_gain[:, 2, 1], C_HEADS), rep(ones, C_HEADS),
    ], axis=1)[:, None, :]
    return {
        "norm_attn": norm_attn, "norm_ffn": norm_ffn, "gain": gain,
        "w_in": w_in.astype(BF16),
        "wb_a": w_branch[:, :A_WIDTH].astype(BF16),
        "wb_b": w_branch[:, A_WIDTH:A_WIDTH + B_WIDTH].astype(BF16),
        "wb_c": w_branch[:, A_WIDTH + B_WIDTH:].astype(BF16),
        "w_out": w_out.astype(BF16),
        "w_rt": jnp.swapaxes(w_router, 1, 2).astype(BF16),
        "c_bias": jax.vmap(_c_bias_table)(rpb),
        "w_gate": w_gate, "w_up": w_up, "w_down": w_down,
    }


def kernel(x_prompt, x_sample, norm_attn, w_in, qk_gain, rpb, w_branch, w_out, norm_ffn, w_router,
           w_gate, w_up, w_down):
    W = _prepare(norm_attn, w_in, qk_gain, rpb, w_branch, w_out, norm_ffn, w_router, w_gate,
                 w_up, w_down)
    depth = w_in.shape[0]
    return _trunk(x_prompt, W, depth), _trunk(x_sample, W, depth)
```

```python
import functools

import jax
import jax.numpy as jnp
from jax import lax
from jax.experimental import pallas as pl
from jax.experimental.pallas import tpu as pltpu

F32 = jnp.float32
BF16 = jnp.bfloat16

D_MODEL = 2048
HEAD_DIM = 128
GRID_W = 64
A_Q_HEADS = 8
A_KV_HEADS = 2
A_GROUP = A_Q_HEADS // A_KV_HEADS
AXIAL_THETA = 10000.0
B_DILATIONS = (1, 4, 16)
B_GROUPS = 3
B_HEADS_PER_GROUP = 4
B_HEADS = B_GROUPS * B_HEADS_PER_GROUP
B_HALF = 64
ROPE_THETA = 500000.0
ROPE_DIMS = HEAD_DIM // 4
C_HEADS = 8
C_WIN_ROWS = 8
C_WIN_COLS = 16
A_WIDTH = A_Q_HEADS * HEAD_DIM
B_WIDTH = B_HEADS_PER_GROUP * HEAD_DIM
C_WIDTH = C_HEADS * HEAD_DIM
QKV_A = (A_Q_HEADS + 2 * A_KV_HEADS) * HEAD_DIM
QKV_B = 3 * B_WIDTH
QKV_C = 3 * C_WIDTH
QKV_WIDTH = QKV_A + B_GROUPS * QKV_B + QKV_C
N_EXPERTS = 16
EXPERT_FF = 2048
EC_CAPACITY = 2
EPS = 1e-6
NEG = -1e30
SCALE = HEAD_DIM ** -0.5
LOG2E = 1.4426950408889634

VMEM_LIMIT = 56 * 1024 * 1024

PLAIN, NORM, ROPE = 0, 1, 2


def _params(*sem):
    return pltpu.CompilerParams(dimension_semantics=sem, vmem_limit_bytes=VMEM_LIMIT)


def _rms_rows(xf, g):
    ms = jnp.mean(xf * xf, axis=-1, keepdims=True)
    return xf * lax.rsqrt(ms + EPS) * g


def _nt_dot(a, b):
    return lax.dot_general(a, b, (((1,), (1,)), ((), ())), preferred_element_type=F32)


def _perm_axial(w):
    lead = w.shape[:-1]
    return w.reshape(*lead, -1, 2, 2, HEAD_DIM // 4).swapaxes(-2, -3).reshape(*lead, -1)


def _perm_partial(w):
    lead = w.shape[:-1]
    h = w.reshape(*lead, -1, HEAD_DIM)
    m = ROPE_DIMS // 2
    out = jnp.concatenate(
        [h[..., :m], h[..., 2 * m:HEAD_DIM // 2 + m], h[..., m:2 * m], h[..., HEAD_DIM // 2 + m:]],
        axis=-1)
    return out.reshape(*lead, -1)


def _angles(pos, dims, theta):
    inv = theta ** (-(jnp.arange(0, dims, 2, dtype=F32) / dims))
    return pos.astype(F32)[:, None] * inv[None, :]


def _axial_tables(L):
    t = jnp.arange(L)
    ra = _angles(t // GRID_W, HEAD_DIM // 2, AXIAL_THETA)
    ca = _angles(t % GRID_W, HEAD_DIM // 2, AXIAL_THETA)
    cos = jnp.concatenate([jnp.cos(ra), jnp.cos(ca), jnp.cos(ra), jnp.cos(ca)], axis=-1)
    sin = jnp.concatenate([-jnp.sin(ra), -jnp.sin(ca), jnp.sin(ra), jnp.sin(ca)], axis=-1)
    return cos, sin


def _partial_tables(L):
    t = jnp.arange(L)
    ang = _angles(t, ROPE_DIMS, ROPE_THETA)
    pad = HEAD_DIM // 2 - ROPE_DIMS // 2
    one, zero = jnp.ones((L, pad), F32), jnp.zeros((L, pad), F32)
    cos = jnp.concatenate([jnp.cos(ang), one, jnp.cos(ang), one], axis=-1)
    sin = jnp.concatenate([-jnp.sin(ang), zero, jnp.sin(ang), zero], axis=-1)
    return cos, sin


def _class_major(tab, dil):
    L = tab.shape[0]
    return tab.reshape(L // dil, dil, HEAD_DIM).transpose(1, 0, 2)


def _norm_kernel(x_ref, g_ref, h_ref, h4_ref, h16_ref, hf_ref, *, tm):
    xf = x_ref[...]
    inv = lax.rsqrt(jnp.mean(xf * xf, axis=-1, keepdims=True) + EPS)
    for j in range(D_MODEL // HEAD_DIM):
        cols = slice(j * HEAD_DIM, (j + 1) * HEAD_DIM)
        y = x_ref[:, cols] * inv * g_ref[:, cols]
        hf_ref[j] = y
        h_ref[:, cols] = y.astype(BF16)
    for dil, ref in ((B_DILATIONS[1], h4_ref), (B_DILATIONS[2], h16_ref)):
        for r in range(dil):
            for j in range(D_MODEL // HEAD_DIM):
                ref[r, :, j * HEAD_DIM:(j + 1) * HEAD_DIM] = hf_ref[
                    j, pl.ds(r, tm // dil, stride=dil), :].astype(BF16)


def _attn_norm(x, g, b, L):
    N = x.shape[0]
    tm = 512
    nl = L // tm
    d1, d2 = B_DILATIONS[1], B_DILATIONS[2]
    row = pl.BlockSpec((tm, D_MODEL), lambda i: (i, 0))
    return pl.pallas_call(
        functools.partial(_norm_kernel, tm=tm),
        grid=(N // tm,),
        in_specs=[row, pl.BlockSpec((1, D_MODEL), lambda i: (0, 0))],
        out_specs=(
            row,
            pl.BlockSpec((None, d1, tm // d1, D_MODEL), lambda i: (i // nl, 0, i % nl, 0)),
            pl.BlockSpec((None, d2, tm // d2, D_MODEL), lambda i: (i // nl, 0, i % nl, 0)),
        ),
        out_shape=(
            jax.ShapeDtypeStruct((N, D_MODEL), BF16),
            jax.ShapeDtypeStruct((b, d1, L // d1, D_MODEL), BF16),
            jax.ShapeDtypeStruct((b, d2, L // d2, D_MODEL), BF16),
        ),
        scratch_shapes=[pltpu.VMEM((D_MODEL // HEAD_DIM, tm, HEAD_DIM), F32)],
        compiler_params=_params("parallel"),
        name="attn_norm",
    )(x, g)


PROJ_TN = 512
PROJ_RC = 256


def _proj_kernel(*refs, tm, kinds, rope):
    if rope:
        h_ref, w_ref, gain_ref, cos_ref, sin_ref, o_ref = refs
    else:
        h_ref, w_ref, gain_ref, o_ref = refs
    heads_per_tile = PROJ_TN // HEAD_DIM
    rc = min(PROJ_RC, tm)
    for c in range(tm // rc):
        rows = slice(c * rc, (c + 1) * rc)
        hc = h_ref[rows, :]
        for t in range(len(kinds) // heads_per_tile):
            acc = jnp.dot(hc, w_ref[:, t * PROJ_TN:(t + 1) * PROJ_TN], preferred_element_type=F32)
            for hd in range(heads_per_tile):
                kind = kinds[t * heads_per_tile + hd]
                y = acc[:, hd * HEAD_DIM:(hd + 1) * HEAD_DIM]
                cols = slice(t * PROJ_TN + hd * HEAD_DIM, t * PROJ_TN + (hd + 1) * HEAD_DIM)
                if kind != PLAIN:
                    y = _rms_rows(y, gain_ref[:, cols])
                if kind == ROPE:
                    y = y * cos_ref[rows, :] + pltpu.roll(y, HEAD_DIM // 2, 1) * sin_ref[rows, :]
                o_ref[rows, cols] = y.astype(BF16)


_KINDS_A = (ROPE,) * (A_Q_HEADS + A_KV_HEADS) + (PLAIN,) * A_KV_HEADS
_KINDS_B = (ROPE,) * 4 + (PLAIN,) * 4 + (ROPE,) * 4
_KINDS_C = (NORM,) * (2 * C_HEADS) + (PLAIN,) * C_HEADS


def _proj_a(h, w_in, layer, gain, cos, sin, L):
    N = h.shape[0]
    tm = min(1024, L)
    nl = L // tm
    tab = pl.BlockSpec((tm, HEAD_DIM), lambda i: (i % nl, 0))
    return pl.pallas_call(
        functools.partial(_proj_kernel, tm=tm, kinds=_KINDS_A, rope=True),
        grid=(N // tm,),
        in_specs=[
            pl.BlockSpec((tm, D_MODEL), lambda i: (i, 0)),
            pl.BlockSpec((None, D_MODEL, QKV_A), lambda i: (layer, 0, 0)),
            pl.BlockSpec((1, QKV_A), lambda i: (0, 0)),
            tab, tab,
        ],
        out_specs=pl.BlockSpec((tm, QKV_A), lambda i: (i, 0)),
        out_shape=jax.ShapeDtypeStruct((N, QKV_A), BF16),
        compiler_params=_params("parallel"),
        name="proj_a",
    )(h, w_in, gain, cos, sin)


def _proj_b(hd, w_in, layer, g, gain, cos, sin):
    b, dil, n, _ = hd.shape
    tm = min(1024, n)
    col0 = QKV_A // QKV_B
    tab = pl.BlockSpec((None, tm, HEAD_DIM), lambda bi, r, i: (r, i, 0))
    return pl.pallas_call(
        functools.partial(_proj_kernel, tm=tm, kinds=_KINDS_B, rope=True),
        grid=(b, dil, n // tm),
        in_specs=[
            pl.BlockSpec((None, None, tm, D_MODEL), lambda bi, r, i: (bi, r, i, 0)),
            pl.BlockSpec((None, D_MODEL, QKV_B), lambda bi, r, i: (layer, 0, col0 + g)),
            pl.BlockSpec((1, QKV_B), lambda bi, r, i: (0, 0)),
            tab, tab,
        ],
        out_specs=pl.BlockSpec((None, None, tm, QKV_B), lambda bi, r, i: (bi, r, i, 0)),
        out_shape=jax.ShapeDtypeStruct((b, dil, n, QKV_B), BF16),
        compiler_params=_params("parallel", "parallel", "parallel"),
        name=f"proj_b{g}",
    )(hd, w_in, gain, cos, sin)


def _proj_c(h, w_in, layer, gain):
    N = h.shape[0]
    tm = 512
    col0 = (QKV_A + B_GROUPS * QKV_B) // QKV_C
    return pl.pallas_call(
        functools.partial(_proj_kernel, tm=tm, kinds=_KINDS_C, rope=False),
        grid=(N // tm,),
        in_specs=[
            pl.BlockSpec((tm, D_MODEL), lambda i: (i, 0)),
            pl.BlockSpec((None, D_MODEL, QKV_C), lambda i: (layer, 0, col0)),
            pl.BlockSpec((1, QKV_C), lambda i: (0, 0)),
        ],
        out_specs=pl.BlockSpec((tm, QKV_C), lambda i: (i, 0)),
        out_shape=jax.ShapeDtypeStruct((N, QKV_C), BF16),
        compiler_params=_params("parallel"),
        name="proj_c",
    )(h, w_in, gain)


A_TQ = 256
A_TK = 512
A_UNROLL = 4


def _attn_a_kernel(q_ref, k_ref, v_ref, o_ref, qs_ref, m_ref, l_ref, acc_ref, *, L):
    tq, tk = A_TQ, A_TK
    c2 = SCALE * LOG2E
    for h in range(A_GROUP):
        qs_ref[h * tq:(h + 1) * tq, :] = q_ref[:, h * HEAD_DIM:(h + 1) * HEAD_DIM]
    m_ref[...] = jnp.full(m_ref.shape, -jnp.inf, F32)
    l_ref[...] = jnp.zeros(l_ref.shape, F32)
    acc_ref[...] = jnp.zeros(acc_ref.shape, F32)

    def body(c, carry):
        c0 = pl.multiple_of(c * tk, tk)
        s = _nt_dot(qs_ref[...], k_ref[pl.ds(c0, tk), :])
        m_prev = m_ref[...]
        m_next = jnp.maximum(m_prev, jnp.max(s, axis=1, keepdims=True))
        p = jnp.exp2((s - pltpu.repeat(m_next, tk // HEAD_DIM, axis=1)) * c2)
        alpha = jnp.exp2((m_prev - m_next) * c2)
        l_ref[...] = alpha * l_ref[...] + jnp.sum(p, axis=1, keepdims=True)
        acc_ref[...] = acc_ref[...] * alpha + jnp.dot(
            p.astype(BF16), v_ref[pl.ds(c0, tk), :], preferred_element_type=F32)
        m_ref[...] = m_next
        return carry

    lax.fori_loop(0, L // tk, body, 0, unroll=A_UNROLL)
    o = acc_ref[...] / l_ref[...]
    for h in range(A_GROUP):
        o_ref[:, h * HEAD_DIM:(h + 1) * HEAD_DIM] = o[h * tq:(h + 1) * tq].astype(BF16)


def _attn_a(qkv3):
    b, L, _ = qkv3.shape
    rows = A_GROUP * A_TQ
    qw = A_GROUP * HEAD_DIM
    return pl.pallas_call(
        functools.partial(_attn_a_kernel, L=L),
        grid=(b, A_KV_HEADS, L // A_TQ),
        in_specs=[
            pl.BlockSpec((None, A_TQ, qw), lambda bi, kv, qi: (bi, qi, kv)),
            pl.BlockSpec((None, L, HEAD_DIM), lambda bi, kv, qi: (bi, 0, A_Q_HEADS + kv)),
            pl.BlockSpec((None, L, HEAD_DIM),
                         lambda bi, kv, qi: (bi, 0, A_Q_HEADS + A_KV_HEADS + kv)),
        ],
        out_specs=pl.BlockSpec((None, A_TQ, qw), lambda bi, kv, qi: (bi, qi, kv)),
        out_shape=jax.ShapeDtypeStruct((b, L, A_WIDTH), BF16),
        scratch_shapes=[
            pltpu.VMEM((rows, HEAD_DIM), BF16),
            pltpu.VMEM((rows, HEAD_DIM), F32),
            pltpu.VMEM((rows, HEAD_DIM), F32),
            pltpu.VMEM((rows, HEAD_DIM), F32),
        ],
        compiler_params=_params("parallel", "parallel", "arbitrary"),
        name="attn_a",
    )(qkv3, qkv3, qkv3)


B_TT = 1024
B_SUB = 128


def _band_tiles(qs, ks, vs, bases, n):
    nq, span = qs[0].shape[0], ks[0].shape[0]
    row = lax.broadcasted_iota(jnp.int32, (nq, span), 0)
    col = lax.broadcasted_iota(jnp.int32, (nq, span), 1)
    band = (col >= row) & (col - row <= 2 * B_HALF)
    ss = [_nt_dot(q, k) * SCALE for q, k in zip(qs, ks)]
    ss = [jnp.where(band & (col + base >= 0) & (col + base < n), s, NEG)
          for s, base in zip(ss, bases)]
    ms = [jnp.max(s, axis=1, keepdims=True) for s in ss]
    ps = [jnp.exp(s - m) for s, m in zip(ss, ms)]
    ls = [jnp.sum(p, axis=1, keepdims=True) for p in ps]
    pns = [(p * (1.0 / l)).astype(BF16) for p, l in zip(ps, ls)]
    os = [jnp.dot(pn, v, preferred_element_type=F32) for pn, v in zip(pns, vs)]
    return os, [m + jnp.log(l) for m, l in zip(ms, ls)]


def _attn_b_kernel(kv0, q0, p0, n0, kv1, q1, p1, n1, kv2, q2, p2, n2, o_ref, os_ref, ls_ref,
                   *, L):
    i = pl.program_id(1)
    d1, d2 = B_DILATIONS[1], B_DILATIONS[2]
    kcol = lambda h: slice(h * HEAD_DIM, (h + 1) * HEAD_DIM)
    vcol = lambda h: slice(B_WIDTH + h * HEAD_DIM, B_WIDTH + (h + 1) * HEAD_DIM)

    pc2 = B_TT // d2

    heads = range(B_HEADS_PER_GROUP)

    def halo_cat(prev, cur, nxt, r, cols):
        return jnp.concatenate([prev[r, :, cols], cur[r, :, cols], nxt[r, :, cols]], axis=0)

    def class2(it, carry):
        rs = [2 * it, 2 * it + 1]
        tiles = [(r, h) for r in rs for h in heads]
        qs = [q2[r, :, kcol(h)] for r, h in tiles]
        ks = [halo_cat(p2, kv2, n2, r, kcol(h)) for r, h in tiles]
        vs = [halo_cat(p2, kv2, n2, r, vcol(h)) for r, h in tiles]
        os, lses = _band_tiles(qs, ks, vs, [i * pc2 - B_HALF] * len(tiles), L // d2)
        for (r, h), o, lse in zip(tiles, os, lses):
            os_ref[1, h, pl.ds(r, pc2, stride=d2), :] = o
            ls_ref[1, h, pl.ds(r, pc2, stride=d2), :] = jnp.broadcast_to(lse, (pc2, HEAD_DIM))
        return carry

    lax.fori_loop(0, d2 // 2, class2, 0)

    pc1 = B_TT // d1
    span = B_SUB + 2 * B_HALF

    def class1(r, carry):
        tiles = [(h, jq * B_SUB) for h in heads for jq in range(pc1 // B_SUB)]
        kf = [halo_cat(p1, kv1, n1, r, kcol(h)) for h in heads]
        vf = [halo_cat(p1, kv1, n1, r, vcol(h)) for h in heads]
        qs = [q1[r, lo:lo + B_SUB, kcol(h)] for h, lo in tiles]
        ks = [kf[h][lo:lo + span] for h, lo in tiles]
        vs = [vf[h][lo:lo + span] for h, lo in tiles]
        os, lses = _band_tiles(qs, ks, vs, [i * pc1 + lo - B_HALF for _, lo in tiles], L // d1)
        for (h, lo), o, lse in zip(tiles, os, lses):
            rows = pl.ds(r + d1 * lo, B_SUB, stride=d1)
            os_ref[0, h, rows, :] = o
            ls_ref[0, h, rows, :] = jnp.broadcast_to(lse, (B_SUB, HEAD_DIM))
        return carry

    lax.fori_loop(0, d1, class1, 0)

    nsub = B_TT // B_SUB
    for jq in range(nsub):
        lo = jq * B_SUB
        rows = slice(lo, lo + B_SUB)

        def window(ref_p, ref_c, ref_n, cols):
            if jq == 0:
                return jnp.concatenate([ref_p[0, :, cols], ref_c[0, :B_SUB + B_HALF, cols]], axis=0)
            if jq == nsub - 1:
                return jnp.concatenate([ref_c[0, lo - B_HALF:, cols], ref_n[0, :, cols]], axis=0)
            return ref_c[0, lo - B_HALF:lo + B_SUB + B_HALF, cols]

        o0s, lse0s = _band_tiles([q0[0, rows, kcol(h)] for h in heads],
                                 [window(p0, kv0, n0, kcol(h)) for h in heads],
                                 [window(p0, kv0, n0, vcol(h)) for h in heads],
                                 [i * B_TT + lo - B_HALF] * len(heads), L)
        for h in heads:
            o0 = o0s[h]
            a0 = jnp.broadcast_to(lse0s[h], (B_SUB, HEAD_DIM))
            a1 = ls_ref[0, h, rows, :]
            a2 = ls_ref[1, h, rows, :]
            m = jnp.maximum(jnp.maximum(a0, a1), a2)
            e0, e1, e2 = jnp.exp(a0 - m), jnp.exp(a1 - m), jnp.exp(a2 - m)
            inv = 1.0 / (e0 + e1 + e2)
            out = (e0 * inv) * o0 + (e1 * inv) * os_ref[0, h, rows, :] \
                + (e2 * inv) * os_ref[1, h, rows, :]
            o_ref[rows, kcol(h)] = out.astype(BF16)


def _attn_b(qkvb, b, L):
    in_specs, args = [], []
    for g, dil in enumerate(B_DILATIONS):
        n = L // dil
        pc = B_TT // dil
        hb = pc // B_HALF
        nhb = n // B_HALF
        in_specs += [
            pl.BlockSpec((None, dil, pc, 2 * B_WIDTH), lambda bi, i: (bi, 0, i, 0)),
            pl.BlockSpec((None, dil, pc, B_WIDTH), lambda bi, i: (bi, 0, i, 2)),
            pl.BlockSpec((None, dil, B_HALF, 2 * B_WIDTH),
                         lambda bi, i, hb=hb: (bi, 0, jnp.maximum(i * hb - 1, 0), 0)),
            pl.BlockSpec((None, dil, B_HALF, 2 * B_WIDTH),
                         lambda bi, i, hb=hb, nhb=nhb: (bi, 0, jnp.minimum((i + 1) * hb, nhb - 1), 0)),
        ]
        args += [qkvb[g]] * 4
    return pl.pallas_call(
        functools.partial(_attn_b_kernel, L=L),
        grid=(b, L // B_TT),
        in_specs=in_specs,
        out_specs=pl.BlockSpec((None, B_TT, B_WIDTH), lambda bi, i: (bi, i, 0)),
        out_shape=jax.ShapeDtypeStruct((b, L, B_WIDTH), BF16),
        scratch_shapes=[pltpu.VMEM((2, B_HEADS_PER_GROUP, B_TT, HEAD_DIM), F32)] * 2,
        compiler_params=_params("parallel", "arbitrary"),
        name="attn_b",
    )(*args)


C_ROWS_PER_ITER = 16


def _c_bias_table(rpb):
    c = jnp.arange(GRID_W)
    cs = jnp.clip(c - C_WIN_COLS // 2, 0, GRID_W - C_WIN_COLS)
    col_ok = (c[None, :] >= cs[:, None]) & (c[None, :] < cs[:, None] + C_WIN_COLS)
    dc = jnp.clip(c[None, :] - c[:, None], -(C_WIN_COLS - 1), C_WIN_COLS - 1) + C_WIN_COLS - 1
    cb = rpb[:, :, dc].astype(F32)
    cb = jnp.where(col_ok[None, None], cb, NEG)
    a = jnp.arange(C_WIN_ROWS)[:, None] + jnp.arange(C_WIN_ROWS)[None, :]
    bt = cb[:, a]
    bt = bt.transpose(0, 1, 3, 2, 4)
    return bt.reshape(C_HEADS, C_WIN_ROWS, GRID_W, C_WIN_ROWS * GRID_W)


def _attn_c_kernel(q_ref, k_ref, v_ref, bt_ref, o_ref, *, rows_per_step, R):
    qi = pl.program_id(2)
    kw = C_WIN_ROWS * GRID_W

    def body(it, carry):
        q0s, k0s, deltas = [], [], []
        for u in range(C_ROWS_PER_ITER):
            rr = it * C_ROWS_PER_ITER + u
            r = qi * rows_per_step + rr
            rs = jnp.clip(r - C_WIN_ROWS // 2, 0, R - C_WIN_ROWS)
            deltas.append(rs - r + (C_WIN_ROWS - 1))
            q0s.append(pl.multiple_of(rr * GRID_W, GRID_W))
            k0s.append(pl.multiple_of(rs * GRID_W, GRID_W))
        qs = [q_ref[pl.ds(q0, GRID_W), :] for q0 in q0s]
        ks = [k_ref[pl.ds(k0, kw), :] for k0 in k0s]
        vs = [v_ref[pl.ds(k0, kw), :] for k0 in k0s]
        bs = [bt_ref[d] for d in deltas]
        ss = [_nt_dot(q, k) * SCALE + bias for q, k, bias in zip(qs, ks, bs)]
        ms = [jnp.max(s, axis=1, keepdims=True) for s in ss]
        ps = [jnp.exp(s - m) for s, m in zip(ss, ms)]
        ls = [jnp.sum(p, axis=1, keepdims=True) for p in ps]
        pns = [(p * (1.0 / l)).astype(BF16) for p, l in zip(ps, ls)]
        outs = [jnp.dot(pn, v, preferred_element_type=F32).astype(BF16) for pn, v in zip(pns, vs)]
        for q0, o in zip(q0s, outs):
            o_ref[pl.ds(q0, GRID_W), :] = o
        return carry

    lax.fori_loop(0, rows_per_step // C_ROWS_PER_ITER, body, 0)


def _attn_c(qkv3, bt):
    b, L, _ = qkv3.shape
    R = L // GRID_W
    assert R >= C_WIN_ROWS
    tq = min(2048, L)
    return pl.pallas_call(
        functools.partial(_attn_c_kernel, rows_per_step=tq // GRID_W, R=R),
        grid=(b, C_HEADS, L // tq),
        in_specs=[
            pl.BlockSpec((None, tq, HEAD_DIM), lambda bi, h, qi: (bi, qi, h)),
            pl.BlockSpec((None, L, HEAD_DIM), lambda bi, h, qi: (bi, 0, C_HEADS + h)),
            pl.BlockSpec((None, L, HEAD_DIM), lambda bi, h, qi: (bi, 0, 2 * C_HEADS + h)),
            pl.BlockSpec((None, C_WIN_ROWS, GRID_W, C_WIN_ROWS * GRID_W),
                         lambda bi, h, qi: (h, 0, 0, 0)),
        ],
        out_specs=pl.BlockSpec((None, tq, HEAD_DIM), lambda bi, h, qi: (bi, qi, h)),
        out_shape=jax.ShapeDtypeStruct((b, L, C_WIDTH), BF16),
        compiler_params=_params("parallel", "parallel", "arbitrary"),
        name="attn_c",
    )(qkv3, qkv3, qkv3, bt)


MERGE_TN = 512


def _merge_kernel(h_ref, oa_ref, ob_ref, oc_ref, wga_ref, wgb_ref, wgc_ref,
                  wba_ref, wbb_ref, wbc_ref, o_ref):
    h = h_ref[...]

    def branch(wg_ref, o_in_ref, wb_ref):
        gate = jax.nn.sigmoid(jnp.dot(h, wg_ref[...], preferred_element_type=F32))
        return gate * jnp.dot(o_in_ref[...], wb_ref[...], preferred_element_type=F32)

    merged = branch(wga_ref, oa_ref, wba_ref)
    merged = merged + branch(wgb_ref, ob_ref, wbb_ref)
    merged = merged + branch(wgc_ref, oc_ref, wbc_ref)
    o_ref[...] = merged.astype(BF16)


def _merge(h, oa, ob, oc, w_in, wba, wbb, wbc, layer):
    N = h.shape[0]
    tm = 512
    tn = MERGE_TN
    gcol = QKV_WIDTH // tn
    dcol = D_MODEL // tn

    def rows(width):
        return pl.BlockSpec((tm, width), lambda i, j: (i, 0))

    def gate_w(k):
        return pl.BlockSpec((None, D_MODEL, tn), lambda i, j: (layer, 0, gcol + k * dcol + j))

    def branch_w(width):
        return pl.BlockSpec((None, width, tn), lambda i, j: (layer, 0, j))

    return pl.pallas_call(
        _merge_kernel,
        grid=(N // tm, dcol),
        in_specs=[
            rows(D_MODEL), rows(A_WIDTH), rows(B_WIDTH), rows(C_WIDTH),
            gate_w(0), gate_w(1), gate_w(2),
            branch_w(A_WIDTH), branch_w(B_WIDTH), branch_w(C_WIDTH),
        ],
        out_specs=pl.BlockSpec((tm, tn), lambda i, j: (i, j)),
        out_shape=jax.ShapeDtypeStruct((N, D_MODEL), BF16),
        compiler_params=_params("parallel", "arbitrary"),
        name="merge",
    )(h, oa, ob, oc, w_in, w_in, w_in, wba, wbb, wbc)


def _out_kernel(m_ref, x_ref, w_ref, g_ref, wr_ref, xo_ref, h_ref, aff_ref):
    xn = x_ref[...] + jnp.dot(m_ref[...], w_ref[...], preferred_element_type=F32)
    xo_ref[...] = xn
    h = _rms_rows(xn, g_ref[...]).astype(BF16)
    h_ref[...] = h
    logits = _nt_dot(wr_ref[...], h)
    mx = jnp.max(logits, axis=0, keepdims=True)
    e = jnp.exp(logits - mx)
    aff_ref[...] = e / jnp.sum(e, axis=0, keepdims=True)


def _out_proj(merged, x, w_out, g, w_rt, layer):
    N = x.shape[0]
    tm = 512
    row = pl.BlockSpec((tm, D_MODEL), lambda i: (i, 0))
    return pl.pallas_call(
        _out_kernel,
        grid=(N // tm,),
        in_specs=[
            row, row,
            pl.BlockSpec((None, D_MODEL, D_MODEL), lambda i: (layer, 0, 0)),
            pl.BlockSpec((1, D_MODEL), lambda i: (0, 0)),
            pl.BlockSpec((None, N_EXPERTS, D_MODEL), lambda i: (layer, 0, 0)),
        ],
        out_specs=(row, row, pl.BlockSpec((N_EXPERTS, tm), lambda i: (0, i))),
        out_shape=(
            jax.ShapeDtypeStruct((N, D_MODEL), F32),
            jax.ShapeDtypeStruct((N, D_MODEL), BF16),
            jax.ShapeDtypeStruct((N_EXPERTS, N), F32),
        ),
        compiler_params=_params("parallel"),
        name="out_proj",
    )(merged, x, w_out, g, w_rt)


def _ffn_kernel(xe_ref, wg_ref, wu_ref, wd_ref, gv_ref, o_ref):
    f = pl.program_id(2)
    xe = xe_ref[...]
    gate = jnp.dot(xe, wg_ref[...].astype(BF16), preferred_element_type=F32)
    up = jnp.dot(xe, wu_ref[...].astype(BF16), preferred_element_type=F32)
    hid = (jax.nn.silu(gate) * up).astype(BF16)
    part = jnp.dot(hid, wd_ref[...].astype(BF16), preferred_element_type=F32)

    @pl.when(f == 0)
    def _():
        o_ref[...] = part

    @pl.when(f > 0)
    def _():
        o_ref[...] += part

    @pl.when(f == pl.num_programs(2) - 1)
    def _():
        o_ref[...] = o_ref[...] * gv_ref[...]


def _ffn(xe, w_gate, w_up, w_down, gval, layer):
    E, cap, _ = xe.shape
    ct = min(1024, cap)
    tf = 256
    nc = cap // ct
    return pl.pallas_call(
        _ffn_kernel,
        grid=(E, nc, EXPERT_FF // tf),
        in_specs=[
            pl.BlockSpec((None, ct, D_MODEL), lambda e, c, f: (e, c, 0)),
            pl.BlockSpec((None, None, D_MODEL, tf), lambda e, c, f: (layer, e, 0, f)),
            pl.BlockSpec((None, None, D_MODEL, tf), lambda e, c, f: (layer, e, 0, f)),
            pl.BlockSpec((None, None, tf, D_MODEL), lambda e, c, f: (layer, e, f, 0)),
            pl.BlockSpec((ct, 1), lambda e, c, f: (e * nc + c, 0)),
        ],
        out_specs=pl.BlockSpec((None, ct, D_MODEL), lambda e, c, f: (e, c, 0)),
        out_shape=jax.ShapeDtypeStruct((E, cap, D_MODEL), F32),
        compiler_params=_params("parallel", "parallel", "arbitrary"),
        name="expert_ffn",
    )(xe, w_gate, w_up, w_down, gval)


def _layer(x, b, L, layer, W, tabs):
    N = b * L
    ax_cos, ax_sin, pr_tabs = tabs
    h, h4, h16 = _attn_norm(x, W["norm_attn"][layer][None], b, L)
    qkv_a = _proj_a(h, W["w_in"], layer, W["gain_a"][layer], ax_cos, ax_sin, L)
    hds = (h.reshape(b, 1, L, D_MODEL), h4, h16)
    qkvb = [_proj_b(hds[g], W["w_in"], layer, g, W["gain_b"][layer], *pr_tabs[g])
            for g in range(B_GROUPS)]
    qkv_c = _proj_c(h, W["w_in"], layer, W["gain_c"][layer])
    oa = _attn_a(qkv_a.reshape(b, L, QKV_A)).reshape(N, A_WIDTH)
    ob = _attn_b(qkvb, b, L).reshape(N, B_WIDTH)
    oc = _attn_c(qkv_c.reshape(b, L, QKV_C), W["c_bias"][layer]).reshape(N, C_WIDTH)
    merged = _merge(h, oa, ob, oc, W["w_in"], W["wb_a"], W["wb_b"], W["wb_c"], layer)
    x, h2, aff_t = _out_proj(merged, x, W["w_out"], W["norm_ffn"][layer][None], W["w_rt"], layer)
    cap = EC_CAPACITY * N // N_EXPERTS
    gval, idx = lax.top_k(aff_t, cap)
    flat = idx.reshape(-1)
    xe = jnp.take(h2, flat, axis=0).reshape(N_EXPERTS, cap, D_MODEL)
    ye = _ffn(xe, W["w_gate"], W["w_up"], W["w_down"], gval.reshape(-1, 1), layer)
    return x.at[flat].add(ye.reshape(-1, D_MODEL))


def _trunk(x3, W, depth):
    b, L, _ = x3.shape
    assert L % B_TT == 0 and L // B_DILATIONS[-1] >= 2 * B_HALF
    ax_cos, ax_sin = _axial_tables(L)
    pr_cos, pr_sin = _partial_tables(L)
    pr_tabs = [(_class_major(pr_cos, d), _class_major(pr_sin, d)) for d in B_DILATIONS]
    x = x3.reshape(b * L, D_MODEL)
    for layer in range(depth):
        x = _layer(x, b, L, layer, W, (ax_cos, ax_sin, pr_tabs))
    return x.reshape(b, L, D_MODEL)


def _prepare(norm_attn, w_in, qk_gain, rpb, w_branch, w_out, norm_ffn, w_router, w_gate, w_up,
             w_down):
    depth = w_in.shape[0]
    ones = jnp.ones((depth, HEAD_DIM), F32)
    rep = lambda v, k: jnp.tile(v, (1, k))
    wb = w_in.astype(BF16)
    seg = lambda lo, width: wb[..., lo:lo + width]
    qa0, ka0, va0 = 0, A_WIDTH, A_WIDTH + A_KV_HEADS * HEAD_DIM
    qb0 = QKV_A
    kb0 = qb0 + B_HEADS * HEAD_DIM
    vb0 = kb0 + B_HEADS * HEAD_DIM
    cols = [_perm_axial(seg(qa0, A_WIDTH)), _perm_axial(seg(ka0, A_KV_HEADS * HEAD_DIM)),
            seg(va0, A_KV_HEADS * HEAD_DIM)]
    for g in range(B_GROUPS):
        cols += [_perm_partial(seg(kb0 + g * B_WIDTH, B_WIDTH)), seg(vb0 + g * B_WIDTH, B_WIDTH),
                 _perm_partial(seg(qb0 + g * B_WIDTH, B_WIDTH))]
    cols.append(wb[..., QKV_A + B_GROUPS * QKV_B:])
    gq_a, gk_a = _perm_axial(qk_gain[:, 0, 0]), _perm_axial(qk_gain[:, 0, 1])
    gq_b, gk_b = _perm_partial(qk_gain[:, 1, 0]), _perm_partial(qk_gain[:, 1, 1])
    hp = B_HEADS_PER_GROUP
    return {
        "norm_attn": norm_attn, "norm_ffn": norm_ffn,
        "gain_a": jnp.concatenate([rep(gq_a, A_Q_HEADS), rep(gk_a, A_KV_HEADS),
                                   rep(ones, A_KV_HEADS)], axis=1)[:, None, :],
        "gain_b": jnp.concatenate([rep(gk_b, hp), rep(ones, hp), rep(gq_b, hp)], axis=1)[:, None, :],
        "gain_c": jnp.concatenate([rep(qk_gain[:, 2, 0], C_HEADS), rep(qk_gain[:, 2, 1], C_HEADS),
                                   rep(ones, C_HEADS)], axis=1)[:, None, :],
        "w_in": jnp.concatenate(cols, axis=-1),
        "wb_a": w_branch[:, :A_WIDTH].astype(BF16),
        "wb_b": w_branch[:, A_WIDTH:A_WIDTH + B_WIDTH].astype(BF16),
        "wb_c": w_branch[:, A_WIDTH + B_WIDTH:].astype(BF16),
        "w_out": w_out.astype(BF16),
        "w_rt": jnp.swapaxes(w_router, 1, 2).astype(BF16),
        "c_bias": jax.vmap(_c_bias_table)(rpb),
        "w_gate": w_gate, "w_up": w_up, "w_down": w_down,
    }


def kernel(x_prompt, x_sample, norm_attn, w_in, qk_gain, rpb, w_branch, w_out, norm_ffn, w_router,
           w_gate, w_up, w_down):
    W = _prepare(norm_attn, w_in, qk_gain, rpb, w_branch, w_out, norm_ffn, w_router, w_gate,
                 w_up, w_down)
    depth = w_in.shape[0]
    return _trunk(x_prompt, W, depth), _trunk(x_sample, W, depth)
```

```python
import functools

import jax
import jax.numpy as jnp
from jax import lax
from jax.experimental import pallas as pl
from jax.experimental.pallas import tpu as pltpu

F32 = jnp.float32
BF16 = jnp.bfloat16

D_MODEL = 2048
HEAD_DIM = 128
GRID_W = 64
A_Q_HEADS = 8
A_KV_HEADS = 2
A_GROUP = A_Q_HEADS // A_KV_HEADS
AXIAL_THETA = 10000.0
B_DILATIONS = (1, 4, 16)
B_GROUPS = 3
B_HEADS_PER_GROUP = 4
B_HEADS = B_GROUPS * B_HEADS_PER_GROUP
B_HALF = 64
ROPE_THETA = 500000.0
ROPE_DIMS = HEAD_DIM // 4
C_HEADS = 8
C_WIN_ROWS = 8
C_WIN_COLS = 16
A_WIDTH = A_Q_HEADS * HEAD_DIM
B_WIDTH = B_HEADS_PER_GROUP * HEAD_DIM
C_WIDTH = C_HEADS * HEAD_DIM
QKV_A = (A_Q_HEADS + 2 * A_KV_HEADS) * HEAD_DIM
QKV_B = 3 * B_WIDTH
QKV_C = 3 * C_WIDTH
QKV_WIDTH = QKV_A + B_GROUPS * QKV_B + QKV_C
N_EXPERTS = 16
EXPERT_FF = 2048
EC_CAPACITY = 2
EPS = 1e-6
NEG = -1e30
SCALE = HEAD_DIM ** -0.5
LOG2E = 1.4426950408889634

VMEM_LIMIT = 56 * 1024 * 1024

PLAIN, NORM, ROPE = 0, 1, 2


def _params(*sem):
    return pltpu.CompilerParams(dimension_semantics=sem, vmem_limit_bytes=VMEM_LIMIT)


def _rms_rows(xf, g):
    ms = jnp.mean(xf * xf, axis=-1, keepdims=True)
    return xf * lax.rsqrt(ms + EPS) * g


def _nt_dot(a, b):
    return lax.dot_general(a, b, (((1,), (1,)), ((), ())), preferred_element_type=F32)


def _perm_axial(w):
    lead = w.shape[:-1]
    return w.reshape(*lead, -1, 2, 2, HEAD_DIM // 4).swapaxes(-2, -3).reshape(*lead, -1)


def _perm_partial(w):
    lead = w.shape[:-1]
    h = w.reshape(*lead, -1, HEAD_DIM)
    m = ROPE_DIMS // 2
    out = jnp.concatenate(
        [h[..., :m], h[..., 2 * m:HEAD_DIM // 2 + m], h[..., m:2 * m], h[..., HEAD_DIM // 2 + m:]],
        axis=-1)
    return out.reshape(*lead, -1)


def _angles(pos, dims, theta):
    inv = theta ** (-(jnp.arange(0, dims, 2, dtype=F32) / dims))
    return pos.astype(F32)[:, None] * inv[None, :]


def _axial_tables(L):
    t = jnp.arange(L)
    ra = _angles(t // GRID_W, HEAD_DIM // 2, AXIAL_THETA)
    ca = _angles(t % GRID_W, HEAD_DIM // 2, AXIAL_THETA)
    cos = jnp.concatenate([jnp.cos(ra), jnp.cos(ca), jnp.cos(ra), jnp.cos(ca)], axis=-1)
    sin = jnp.concatenate([-jnp.sin(ra), -jnp.sin(ca), jnp.sin(ra), jnp.sin(ca)], axis=-1)
    return cos, sin


def _partial_tables(L):
    t = jnp.arange(L)
    ang = _angles(t, ROPE_DIMS, ROPE_THETA)
    pad = HEAD_DIM // 2 - ROPE_DIMS // 2
    one, zero = jnp.ones((L, pad), F32), jnp.zeros((L, pad), F32)
    cos = jnp.concatenate([jnp.cos(ang), one, jnp.cos(ang), one], axis=-1)
    sin = jnp.concatenate([-jnp.sin(ang), zero, jnp.sin(ang), zero], axis=-1)
    return cos, sin


def _class_major(tab, dil):
    L = tab.shape[0]
    return tab.reshape(L // dil, dil, HEAD_DIM).transpose(1, 0, 2)


def _norm_kernel(x_ref, g_ref, h_ref, h4_ref, h16_ref, hf_ref, *, tm):
    xf = x_ref[...]
    inv = lax.rsqrt(jnp.mean(xf * xf, axis=-1, keepdims=True) + EPS)
    for j in range(D_MODEL // HEAD_DIM):
        cols = slice(j * HEAD_DIM, (j + 1) * HEAD_DIM)
        y = x_ref[:, cols] * inv * g_ref[:, cols]
        hf_ref[j] = y
        h_ref[:, cols] = y.astype(BF16)
    for dil, ref in ((B_DILATIONS[1], h4_ref), (B_DILATIONS[2], h16_ref)):
        for r in range(dil):
            for j in range(D_MODEL // HEAD_DIM):
                ref[r, :, j * HEAD_DIM:(j + 1) * HEAD_DIM] = hf_ref[
                    j, pl.ds(r, tm // dil, stride=dil), :].astype(BF16)


def _attn_norm(x, g, b, L):
    N = x.shape[0]
    tm = 512
    nl = L // tm
    d1, d2 = B_DILATIONS[1], B_DILATIONS[2]
    row = pl.BlockSpec((tm, D_MODEL), lambda i: (i, 0))
    return pl.pallas_call(
        functools.partial(_norm_kernel, tm=tm),
        grid=(N // tm,),
        in_specs=[row, pl.BlockSpec((1, D_MODEL), lambda i: (0, 0))],
        out_specs=(
            row,
            pl.BlockSpec((None, d1, tm // d1, D_MODEL), lambda i: (i // nl, 0, i % nl, 0)),
            pl.BlockSpec((None, d2, tm // d2, D_MODEL), lambda i: (i // nl, 0, i % nl, 0)),
        ),
        out_shape=(
            jax.ShapeDtypeStruct((N, D_MODEL), BF16),
            jax.ShapeDtypeStruct((b, d1, L // d1, D_MODEL), BF16),
            jax.ShapeDtypeStruct((b, d2, L // d2, D_MODEL), BF16),
        ),
        scratch_shapes=[pltpu.VMEM((D_MODEL // HEAD_DIM, tm, HEAD_DIM), F32)],
        compiler_params=_params("parallel"),
        name="attn_norm",
    )(x, g)


PROJ_TN = 512
PROJ_RC = 256


def _proj_kernel(*refs, tm, kinds, rope):
    if rope:
        h_ref, w_ref, gain_ref, cos_ref, sin_ref, o_ref = refs
    else:
        h_ref, w_ref, gain_ref, o_ref = refs
    heads_per_tile = PROJ_TN // HEAD_DIM
    rc = min(PROJ_RC, tm)
    for c in range(tm // rc):
        rows = slice(c * rc, (c + 1) * rc)
        hc = h_ref[rows, :]
        for t in range(len(kinds) // heads_per_tile):
            acc = jnp.dot(hc, w_ref[:, t * PROJ_TN:(t + 1) * PROJ_TN], preferred_element_type=F32)
            for hd in range(heads_per_tile):
                kind = kinds[t * heads_per_tile + hd]
                y = acc[:, hd * HEAD_DIM:(hd + 1) * HEAD_DIM]
                cols = slice(t * PROJ_TN + hd * HEAD_DIM, t * PROJ_TN + (hd + 1) * HEAD_DIM)
                if kind != PLAIN:
                    y = _rms_rows(y, gain_ref[:, cols])
                if kind == ROPE:
                    y = y * cos_ref[rows, :] + pltpu.roll(y, HEAD_DIM // 2, 1) * sin_ref[rows, :]
                o_ref[rows, cols] = y.astype(BF16)


_KINDS_A = (ROPE,) * (A_Q_HEADS + A_KV_HEADS) + (PLAIN,) * A_KV_HEADS
_KINDS_B = (ROPE,) * 4 + (PLAIN,) * 4 + (ROPE,) * 4
_KINDS_C = (NORM,) * (2 * C_HEADS) + (PLAIN,) * C_HEADS


def _proj_a(h, w_in, layer, gain, cos, sin, L):
    N = h.shape[0]
    tm = min(1024, L)
    nl = L // tm
    tab = pl.BlockSpec((tm, HEAD_DIM), lambda i: (i % nl, 0))
    return pl.pallas_call(
        functools.partial(_proj_kernel, tm=tm, kinds=_KINDS_A, rope=True),
        grid=(N // tm,),
        in_specs=[
            pl.BlockSpec((tm, D_MODEL), lambda i: (i, 0)),
            pl.BlockSpec((None, D_MODEL, QKV_A), lambda i: (layer, 0, 0)),
            pl.BlockSpec((1, QKV_A), lambda i: (0, 0)),
            tab, tab,
        ],
        out_specs=pl.BlockSpec((tm, QKV_A), lambda i: (i, 0)),
        out_shape=jax.ShapeDtypeStruct((N, QKV_A), BF16),
        compiler_params=_params("parallel"),
        name="proj_a",
    )(h, w_in, gain, cos, sin)


def _proj_b(hd, w_in, layer, g, gain, cos, sin):
    b, dil, n, _ = hd.shape
    tm = min(1024, n)
    col0 = QKV_A // QKV_B
    tab = pl.BlockSpec((None, tm, HEAD_DIM), lambda bi, r, i: (r, i, 0))
    return pl.pallas_call(
        functools.partial(_proj_kernel, tm=tm, kinds=_KINDS_B, rope=True),
        grid=(b, dil, n // tm),
        in_specs=[
            pl.BlockSpec((None, None, tm, D_MODEL), lambda bi, r, i: (bi, r, i, 0)),
            pl.BlockSpec((None, D_MODEL, QKV_B), lambda bi, r, i: (layer, 0, col0 + g)),
            pl.BlockSpec((1, QKV_B), lambda bi, r, i: (0, 0)),
            tab, tab,
        ],
        out_specs=pl.BlockSpec((None, None, tm, QKV_B), lambda bi, r, i: (bi, r, i, 0)),
        out_shape=jax.ShapeDtypeStruct((b, dil, n, QKV_B), BF16),
        compiler_params=_params("parallel", "parallel", "parallel"),
        name=f"proj_b{g}",
    )(hd, w_in, gain, cos, sin)


def _proj_c(h, w_in, layer, gain):
    N = h.shape[0]
    tm = 512
    col0 = (QKV_A + B_GROUPS * QKV_B) // QKV_C
    return pl.pallas_call(
        functools.partial(_proj_kernel, tm=tm, kinds=_KINDS_C, rope=False),
        grid=(N // tm,),
        in_specs=[
            pl.BlockSpec((tm, D_MODEL), lambda i: (i, 0)),
            pl.BlockSpec((None, D_MODEL, QKV_C), lambda i: (layer, 0, col0)),
            pl.BlockSpec((1, QKV_C), lambda i: (0, 0)),
        ],
        out_specs=pl.BlockSpec((tm, QKV_C), lambda i: (i, 0)),
        out_shape=jax.ShapeDtypeStruct((N, QKV_C), BF16),
        compiler_params=_params("parallel"),
        name="proj_c",
    )(h, w_in, gain)


A_TQ = 256
A_TK = 512
A_UNROLL = 4


def _attn_a_kernel(q_ref, k_ref, v_ref, o_ref, qs_ref, m_ref, l_ref, acc_ref, *, L):
    tq, tk = A_TQ, A_TK
    c2 = SCALE * LOG2E
    for h in range(A_GROUP):
        qs_ref[h * tq:(h + 1) * tq, :] = q_ref[:, h * HEAD_DIM:(h + 1) * HEAD_DIM]
    m_ref[...] = jnp.full(m_ref.shape, -jnp.inf, F32)
    l_ref[...] = jnp.zeros(l_ref.shape, F32)
    acc_ref[...] = jnp.zeros(acc_ref.shape, F32)

    def body(c, carry):
        c0 = pl.multiple_of(c * tk, tk)
        s = _nt_dot(qs_ref[...], k_ref[pl.ds(c0, tk), :])
        m_prev = m_ref[...]
        m_next = jnp.maximum(m_prev, jnp.max(s, axis=1, keepdims=True))
        p = jnp.exp2((s - jnp.concatenate([m_next] * (tk // HEAD_DIM), axis=1)) * c2)
        alpha = jnp.exp2((m_prev - m_next) * c2)
        l_ref[...] = alpha * l_ref[...] + jnp.sum(p, axis=1, keepdims=True)
        acc_ref[...] = acc_ref[...] * alpha + jnp.dot(
            p.astype(BF16), v_ref[pl.ds(c0, tk), :], preferred_element_type=F32)
        m_ref[...] = m_next
        return carry

    lax.fori_loop(0, L // tk, body, 0, unroll=A_UNROLL)
    o = acc_ref[...] / l_ref[...]
    for h in range(A_GROUP):
        o_ref[:, h * HEAD_DIM:(h + 1) * HEAD_DIM] = o[h * tq:(h + 1) * tq].astype(BF16)


def _attn_a(qkv3):
    b, L, _ = qkv3.shape
    rows = A_GROUP * A_TQ
    qw = A_GROUP * HEAD_DIM
    return pl.pallas_call(
        functools.partial(_attn_a_kernel, L=L),
        grid=(b, A_KV_HEADS, L // A_TQ),
        in_specs=[
            pl.BlockSpec((None, A_TQ, qw), lambda bi, kv, qi: (bi, qi, kv)),
            pl.BlockSpec((None, L, HEAD_DIM), lambda bi, kv, qi: (bi, 0, A_Q_HEADS + kv)),
            pl.BlockSpec((None, L, HEAD_DIM),
                         lambda bi, kv, qi: (bi, 0, A_Q_HEADS + A_KV_HEADS + kv)),
        ],
        out_specs=pl.BlockSpec((None, A_TQ, qw), lambda bi, kv, qi: (bi, qi, kv)),
        out_shape=jax.ShapeDtypeStruct((b, L, A_WIDTH), BF16),
        scratch_shapes=[
            pltpu.VMEM((rows, HEAD_DIM), BF16),
            pltpu.VMEM((rows, HEAD_DIM), F32),
            pltpu.VMEM((rows, HEAD_DIM), F32),
            pltpu.VMEM((rows, HEAD_DIM), F32),
        ],
        compiler_params=_params("parallel", "parallel", "arbitrary"),
        name="attn_a",
    )(qkv3, qkv3, qkv3)


B_TT = 1024
B_SUB = 128


def _band_tiles(qs, ks, vs, bases, n):
    nq, span = qs[0].shape[0], ks[0].shape[0]
    row = lax.broadcasted_iota(jnp.int32, (nq, span), 0)
    col = lax.broadcasted_iota(jnp.int32, (nq, span), 1)
    band = (col >= row) & (col - row <= 2 * B_HALF)
    ss = [_nt_dot(q, k) * SCALE for q, k in zip(qs, ks)]
    ss = [jnp.where(band & (col + base >= 0) & (col + base < n), s, NEG)
          for s, base in zip(ss, bases)]
    ms = [jnp.max(s, axis=1, keepdims=True) for s in ss]
    ps = [jnp.exp(s - m) for s, m in zip(ss, ms)]
    ls = [jnp.sum(p, axis=1, keepdims=True) for p in ps]
    pns = [(p * (1.0 / l)).astype(BF16) for p, l in zip(ps, ls)]
    os = [jnp.dot(pn, v, preferred_element_type=F32) for pn, v in zip(pns, vs)]
    return os, [m + jnp.log(l) for m, l in zip(ms, ls)]


def _attn_b_kernel(kv0, q0, p0, n0, kv1, q1, p1, n1, kv2, q2, p2, n2, o_ref, os_ref, ls_ref,
                   *, L):
    i = pl.program_id(1)
    d1, d2 = B_DILATIONS[1], B_DILATIONS[2]
    kcol = lambda h: slice(h * HEAD_DIM, (h + 1) * HEAD_DIM)
    vcol = lambda h: slice(B_WIDTH + h * HEAD_DIM, B_WIDTH + (h + 1) * HEAD_DIM)

    pc2 = B_TT // d2

    heads = range(B_HEADS_PER_GROUP)

    def halo_cat(prev, cur, nxt, r, cols):
        return jnp.concatenate([prev[r, :, cols], cur[r, :, cols], nxt[r, :, cols]], axis=0)

    def class2(it, carry):
        rs = [2 * it, 2 * it + 1]
        tiles = [(r, h) for r in rs for h in heads]
        qs = [q2[r, :, kcol(h)] for r, h in tiles]
        ks = [halo_cat(p2, kv2, n2, r, kcol(h)) for r, h in tiles]
        vs = [halo_cat(p2, kv2, n2, r, vcol(h)) for r, h in tiles]
        os, lses = _band_tiles(qs, ks, vs, [i * pc2 - B_HALF] * len(tiles), L // d2)
        for (r, h), o, lse in zip(tiles, os, lses):
            os_ref[1, h, pl.ds(r, pc2, stride=d2), :] = o
            ls_ref[1, h, pl.ds(r, pc2, stride=d2), :] = jnp.broadcast_to(lse, (pc2, HEAD_DIM))
        return carry

    lax.fori_loop(0, d2 // 2, class2, 0)

    pc1 = B_TT // d1
    span = B_SUB + 2 * B_HALF

    def class1(r, carry):
        tiles = [(h, jq * B_SUB) for h in heads for jq in range(pc1 // B_SUB)]
        kf = [halo_cat(p1, kv1, n1, r, kcol(h)) for h in heads]
        vf = [halo_cat(p1, kv1, n1, r, vcol(h)) for h in heads]
        qs = [q1[r, lo:lo + B_SUB, kcol(h)] for h, lo in tiles]
        ks = [kf[h][lo:lo + span] for h, lo in tiles]
        vs = [vf[h][lo:lo + span] for h, lo in tiles]
        os, lses = _band_tiles(qs, ks, vs, [i * pc1 + lo - B_HALF for _, lo in tiles], L // d1)
        for (h, lo), o, lse in zip(tiles, os, lses):
            rows = pl.ds(r + d1 * lo, B_SUB, stride=d1)
            os_ref[0, h, rows, :] = o
            ls_ref[0, h, rows, :] = jnp.broadcast_to(lse, (B_SUB, HEAD_DIM))
        return carry

    lax.fori_loop(0, d1, class1, 0)

    nsub = B_TT // B_SUB
    for jq in range(nsub):
        lo = jq * B_SUB
        rows = slice(lo, lo + B_SUB)

        def window(ref_p, ref_c, ref_n, cols):
            if jq == 0:
                return jnp.concatenate([ref_p[0, :, cols], ref_c[0, :B_SUB + B_HALF, cols]], axis=0)
            if jq == nsub - 1:
                return jnp.concatenate([ref_c[0, lo - B_HALF:, cols], ref_n[0, :, cols]], axis=0)
            return ref_c[0, lo - B_HALF:lo + B_SUB + B_HALF, cols]

        o0s, lse0s = _band_tiles([q0[0, rows, kcol(h)] for h in heads],
                                 [window(p0, kv0, n0, kcol(h)) for h in heads],
                                 [window(p0, kv0, n0, vcol(h)) for h in heads],
                                 [i * B_TT + lo - B_HALF] * len(heads), L)
        for h in heads:
            o0 = o0s[h]
            a0 = jnp.broadcast_to(lse0s[h], (B_SUB, HEAD_DIM))
            a1 = ls_ref[0, h, rows, :]
            a2 = ls_ref[1, h, rows, :]
            m = jnp.maximum(jnp.maximum(a0, a1), a2)
            e0, e1, e2 = jnp.exp(a0 - m), jnp.exp(a1 - m), jnp.exp(a2 - m)
            inv = 1.0 / (e0 + e1 + e2)
            out = (e0 * inv) * o0 + (e1 * inv) * os_ref[0, h, rows, :] \
                + (e2 * inv) * os_ref[1, h, rows, :]
            o_ref[rows, kcol(h)] = out.astype(BF16)


def _attn_b(qkvb, b, L):
    in_specs, args = [], []
    for g, dil in enumerate(B_DILATIONS):
        n = L // dil
        pc = B_TT // dil
        hb = pc // B_HALF
        nhb = n // B_HALF
        in_specs += [
            pl.BlockSpec((None, dil, pc, 2 * B_WIDTH), lambda bi, i: (bi, 0, i, 0)),
            pl.BlockSpec((None, dil, pc, B_WIDTH), lambda bi, i: (bi, 0, i, 2)),
            pl.BlockSpec((None, dil, B_HALF, 2 * B_WIDTH),
                         lambda bi, i, hb=hb: (bi, 0, jnp.maximum(i * hb - 1, 0), 0)),
            pl.BlockSpec((None, dil, B_HALF, 2 * B_WIDTH),
                         lambda bi, i, hb=hb, nhb=nhb: (bi, 0, jnp.minimum((i + 1) * hb, nhb - 1), 0)),
        ]
        args += [qkvb[g]] * 4
    return pl.pallas_call(
        functools.partial(_attn_b_kernel, L=L),
        grid=(b, L // B_TT),
        in_specs=in_specs,
        out_specs=pl.BlockSpec((None, B_TT, B_WIDTH), lambda bi, i: (bi, i, 0)),
        out_shape=jax.ShapeDtypeStruct((b, L, B_WIDTH), BF16),
        scratch_shapes=[pltpu.VMEM((2, B_HEADS_PER_GROUP, B_TT, HEAD_DIM), F32)] * 2,
        compiler_params=_params("parallel", "arbitrary"),
        name="attn_b",
    )(*args)


C_ROWS_PER_ITER = 16


def _c_bias_table(rpb):
    c = jnp.arange(GRID_W)
    cs = jnp.clip(c - C_WIN_COLS // 2, 0, GRID_W - C_WIN_COLS)
    col_ok = (c[None, :] >= cs[:, None]) & (c[None, :] < cs[:, None] + C_WIN_COLS)
    dc = jnp.clip(c[None, :] - c[:, None], -(C_WIN_COLS - 1), C_WIN_COLS - 1) + C_WIN_COLS - 1
    cb = rpb[:, :, dc].astype(F32)
    cb = jnp.where(col_ok[None, None], cb, NEG)
    a = jnp.arange(C_WIN_ROWS)[:, None] + jnp.arange(C_WIN_ROWS)[None, :]
    bt = cb[:, a]
    bt = bt.transpose(0, 1, 3, 2, 4)
    return bt.reshape(C_HEADS, C_WIN_ROWS, GRID_W, C_WIN_ROWS * GRID_W)


def _attn_c_kernel(q_ref, k_ref, v_ref, bt_ref, o_ref, *, rows_per_step, R):
    qi = pl.program_id(2)
    kw = C_WIN_ROWS * GRID_W

    def body(it, carry):
        q0s, k0s, deltas = [], [], []
        for u in range(C_ROWS_PER_ITER):
            rr = it * C_ROWS_PER_ITER + u
            r = qi * rows_per_step + rr
            rs = jnp.clip(r - C_WIN_ROWS // 2, 0, R - C_WIN_ROWS)
            deltas.append(rs - r + (C_WIN_ROWS - 1))
            q0s.append(pl.multiple_of(rr * GRID_W, GRID_W))
            k0s.append(pl.multiple_of(rs * GRID_W, GRID_W))
        qs = [q_ref[pl.ds(q0, GRID_W), :] for q0 in q0s]
        ks = [k_ref[pl.ds(k0, kw), :] for k0 in k0s]
        vs = [v_ref[pl.ds(k0, kw), :] for k0 in k0s]
        bs = [bt_ref[d] for d in deltas]
        ss = [_nt_dot(q, k) * SCALE + bias for q, k, bias in zip(qs, ks, bs)]
        ms = [jnp.max(s, axis=1, keepdims=True) for s in ss]
        ps = [jnp.exp(s - m) for s, m in zip(ss, ms)]
        ls = [jnp.sum(p, axis=1, keepdims=True) for p in ps]
        pns = [(p * (1.0 / l)).astype(BF16) for p, l in zip(ps, ls)]
        outs = [jnp.dot(pn, v, preferred_element_type=F32).astype(BF16) for pn, v in zip(pns, vs)]
        for q0, o in zip(q0s, outs):
            o_ref[pl.ds(q0, GRID_W), :] = o
        return carry

    lax.fori_loop(0, rows_per_step // C_ROWS_PER_ITER, body, 0)


def _attn_c(qkv3, bt):
    b, L, _ = qkv3.shape
    R = L // GRID_W
    assert R >= C_WIN_ROWS
    tq = min(2048, L)
    return pl.pallas_call(
        functools.partial(_attn_c_kernel, rows_per_step=tq // GRID_W, R=R),
        grid=(b, C_HEADS, L // tq),
        in_specs=[
            pl.BlockSpec((None, tq, HEAD_DIM), lambda bi, h, qi: (bi, qi, h)),
            pl.BlockSpec((None, L, HEAD_DIM), lambda bi, h, qi: (bi, 0, C_HEADS + h)),
            pl.BlockSpec((None, L, HEAD_DIM), lambda bi, h, qi: (bi, 0, 2 * C_HEADS + h)),
            pl.BlockSpec((None, C_WIN_ROWS, GRID_W, C_WIN_ROWS * GRID_W),
                         lambda bi, h, qi: (h, 0, 0, 0)),
        ],
        out_specs=pl.BlockSpec((None, tq, HEAD_DIM), lambda bi, h, qi: (bi, qi, h)),
        out_shape=jax.ShapeDtypeStruct((b, L, C_WIDTH), BF16),
        compiler_params=_params("parallel", "parallel", "arbitrary"),
        name="attn_c",
    )(qkv3, qkv3, qkv3, bt)


MERGE_TN = 512


def _merge_kernel(h_ref, oa_ref, ob_ref, oc_ref, wga_ref, wgb_ref, wgc_ref,
                  wba_ref, wbb_ref, wbc_ref, o_ref):
    h = h_ref[...]

    def branch(wg_ref, o_in_ref, wb_ref):
        gate = jax.nn.sigmoid(jnp.dot(h, wg_ref[...], preferred_element_type=F32))
        return gate * jnp.dot(o_in_ref[...], wb_ref[...], preferred_element_type=F32)

    merged = branch(wga_ref, oa_ref, wba_ref)
    merged = merged + branch(wgb_ref, ob_ref, wbb_ref)
    merged = merged + branch(wgc_ref, oc_ref, wbc_ref)
    o_ref[...] = merged.astype(BF16)


def _merge(h, oa, ob, oc, w_in, wba, wbb, wbc, layer):
    N = h.shape[0]
    tm = 512
    tn = MERGE_TN
    gcol = QKV_WIDTH // tn
    dcol = D_MODEL // tn

    def rows(width):
        return pl.BlockSpec((tm, width), lambda i, j: (i, 0))

    def gate_w(k):
        return pl.BlockSpec((None, D_MODEL, tn), lambda i, j: (layer, 0, gcol + k * dcol + j))

    def branch_w(width):
        return pl.BlockSpec((None, width, tn), lambda i, j: (layer, 0, j))

    return pl.pallas_call(
        _merge_kernel,
        grid=(N // tm, dcol),
        in_specs=[
            rows(D_MODEL), rows(A_WIDTH), rows(B_WIDTH), rows(C_WIDTH),
            gate_w(0), gate_w(1), gate_w(2),
            branch_w(A_WIDTH), branch_w(B_WIDTH), branch_w(C_WIDTH),
        ],
        out_specs=pl.BlockSpec((tm, tn), lambda i, j: (i, j)),
        out_shape=jax.ShapeDtypeStruct((N, D_MODEL), BF16),
        compiler_params=_params("parallel", "arbitrary"),
        name="merge",
    )(h, oa, ob, oc, w_in, w_in, w_in, wba, wbb, wbc)


HALF_D = D_MODEL // 2
HI16 = 0xFFFF0000


def _pack_bf16_pairs(hb):
    bits = lax.bitcast_convert_type(hb.astype(F32), jnp.uint32)
    return (bits[:, HALF_D:] & jnp.uint32(HI16)) | (bits[:, :HALF_D] >> jnp.uint32(16))


def _unpack_bf16_pairs(pk):
    lo = lax.bitcast_convert_type(pk << jnp.uint32(16), F32)
    hi = lax.bitcast_convert_type(pk & jnp.uint32(HI16), F32)
    return lo.astype(BF16), hi.astype(BF16)


def _out_kernel(m_ref, x_ref, w_ref, g_ref, wr_ref, xo_ref, h_ref, aff_ref):
    xn = x_ref[...] + jnp.dot(m_ref[...], w_ref[...], preferred_element_type=F32)
    xo_ref[...] = xn
    h = _rms_rows(xn, g_ref[...]).astype(BF16)
    h_ref[...] = _pack_bf16_pairs(h)
    logits = _nt_dot(wr_ref[...], h)
    mx = jnp.max(logits, axis=0, keepdims=True)
    e = jnp.exp(logits - mx)
    aff_ref[...] = e / jnp.sum(e, axis=0, keepdims=True)


def _out_proj(merged, x, w_out, g, w_rt, layer):
    N = x.shape[0]
    tm = 512
    row = pl.BlockSpec((tm, D_MODEL), lambda i: (i, 0))
    return pl.pallas_call(
        _out_kernel,
        grid=(N // tm,),
        in_specs=[
            row, row,
            pl.BlockSpec((None, D_MODEL, D_MODEL), lambda i: (layer, 0, 0)),
            pl.BlockSpec((1, D_MODEL), lambda i: (0, 0)),
            pl.BlockSpec((None, N_EXPERTS, D_MODEL), lambda i: (layer, 0, 0)),
        ],
        out_specs=(row, pl.BlockSpec((tm, HALF_D), lambda i: (i, 0)),
                   pl.BlockSpec((N_EXPERTS, tm), lambda i: (0, i))),
        out_shape=(
            jax.ShapeDtypeStruct((N, D_MODEL), F32),
            jax.ShapeDtypeStruct((N, HALF_D), jnp.uint32),
            jax.ShapeDtypeStruct((N_EXPERTS, N), F32),
        ),
        compiler_params=_params("parallel"),
        name="out_proj",
    )(merged, x, w_out, g, w_rt)


MOE_CT = 1024
MOE_TF = 256
MOE_TN = 512
MOE_NF = EXPERT_FF // MOE_TF
MOE_NN = D_MODEL // MOE_TN
MOE_NJ = MOE_NF + MOE_NN
MOE_CHUNKS = 4


def _moe_kernel(idx_prev, idx_cur, idx_next, h2_hbm, wg_ref, wu_ref, wd_ref, gv_ref, x_in_hbm,
                x_hbm, xe_ref, xeb_ref, hid_ref, ye_ref, xb_ref, gsem, rsem, wsem,
                *, ct, n_tiles):
    del x_in_hbm
    t = pl.program_id(0)
    j = pl.program_id(1)
    slot = t % 2
    pslot = 1 - slot
    rb = ct // MOE_CHUNKS
    gr = ct // MOE_NF

    def xe_row_copy(idx_ref, dst_slot, r):
        return pltpu.make_async_copy(h2_hbm.at[pl.ds(idx_ref[0, r], 1)],
                                     xe_ref.at[dst_slot, pl.ds(r, 1)], gsem.at[dst_slot])

    def xe_all_rows(dst_slot):
        return pltpu.make_async_copy(h2_hbm.at[pl.ds(0, ct)], xe_ref.at[dst_slot],
                                     gsem.at[dst_slot])

    def x_row_in(k, buf, r):
        return pltpu.make_async_copy(x_hbm.at[pl.ds(idx_prev[0, k * rb + r], 1)],
                                     xb_ref.at[buf, pl.ds(r, 1)], rsem.at[buf])

    def x_row_out(k, buf, r):
        return pltpu.make_async_copy(xb_ref.at[buf, pl.ds(r, 1)],
                                     x_hbm.at[pl.ds(idx_prev[0, k * rb + r], 1)], wsem.at[buf])

    def x_chunk_in(buf):
        return pltpu.make_async_copy(x_hbm.at[pl.ds(0, rb)], xb_ref.at[buf], rsem.at[buf])

    def x_chunk_out(buf):
        return pltpu.make_async_copy(xb_ref.at[buf], x_hbm.at[pl.ds(0, rb)], wsem.at[buf])

    def start_gather_x(k, buf):
        for r in range(rb):
            x_row_in(k, buf, r).start()

    def add_and_scatter(k, buf):
        x_chunk_in(buf).wait()
        r0 = pl.multiple_of(k * rb, rb)
        for n in range(MOE_NN):
            cols = slice(n * MOE_TN, (n + 1) * MOE_TN)
            xb_ref[buf, :, cols] = xb_ref[buf, :, cols] + ye_ref[pslot, n, pl.ds(r0, rb), :]
        for r in range(rb):
            x_row_out(k, buf, r).start()

    def scatter_add_step(kind):
        buf = j % 2
        if kind == "gather":
            start_gather_x(j, buf)
        elif kind == "add":
            add_and_scatter(jnp.where(j < 4, j - 2, j - 4), buf)
        elif kind == "regather":
            x_chunk_out(buf).wait()
            start_gather_x(j - 2, buf)
        elif kind == "drain":
            x_chunk_out(buf).wait()

    @pl.when((t == 0) & (j == 0))
    def _():
        ye_ref[1] = jnp.zeros(ye_ref.shape[1:], F32)

        def body(r, carry):
            xe_row_copy(idx_cur, 0, r).start()
            return carry
        lax.fori_loop(0, ct, body, 0)

    def gate_up_step(kind, first):
        if first:
            xe_all_rows(slot).wait()
            lo, hi = _unpack_bf16_pairs(xe_ref[slot])
            xeb_ref[:, :HALF_D] = lo
            xeb_ref[:, HALF_D:] = hi
        scatter_add_step(kind)
        xe = xeb_ref[...]
        gate = jnp.dot(xe, wg_ref[...], preferred_element_type=F32)
        up = jnp.dot(xe, wu_ref[...], preferred_element_type=F32)
        hid_ref[j] = (jax.nn.silu(gate) * up).astype(BF16)
        for r in range(gr):
            xe_row_copy(idx_next, pslot, j * gr + r).start()

    def down_step(kind):
        scatter_add_step(kind)
        acc = jnp.dot(hid_ref[0], wd_ref[:MOE_TF, :], preferred_element_type=F32)
        for f in range(1, MOE_NF):
            acc = acc + jnp.dot(hid_ref[f], wd_ref[f * MOE_TF:(f + 1) * MOE_TF, :],
                                preferred_element_type=F32)
        ye_ref[slot, j - MOE_NF] = acc * gv_ref[...]

    real = t < n_tiles
    steps = (
        (j == 0, "gather", functools.partial(gate_up_step, "gather", True)),
        (j == 1, "gather", functools.partial(gate_up_step, "gather", False)),
        ((j == 2) | (j == 3) | (j == 6) | (j == 7), "add",
         functools.partial(gate_up_step, "add", False)),
        ((j == 4) | (j == 5), "regather", functools.partial(gate_up_step, "regather", False)),
        ((j == 8) | (j == 9), "drain", functools.partial(down_step, "drain")),
        (j >= 10, None, functools.partial(down_step, None)),
    )
    for cond, kind, fn in steps:
        pl.when(real & cond)(fn)
        if kind is not None:
            pl.when(jnp.logical_not(real) & cond)(functools.partial(scatter_add_step, kind))

    @pl.when(jnp.logical_not(real) & (j == 0))
    def _():
        xe_all_rows(slot).wait()


def _moe(x, h2, idx, gval, w_gate, w_up, w_down, layer):
    N = x.shape[0]
    E, cap = idx.shape
    ct = min(MOE_CT, cap)
    nc = cap // ct
    n_tiles = E * nc
    idx3 = idx.reshape(n_tiles, 1, ct)
    last = n_tiles - 1
    tile = lambda t: jnp.minimum(t, last)

    def idx_spec(off):
        return pl.BlockSpec((None, 1, ct), lambda t, j: (jnp.clip(t + off, 0, last), 0, 0),
                            memory_space=pltpu.SMEM)

    return pl.pallas_call(
        functools.partial(_moe_kernel, ct=ct, n_tiles=n_tiles),
        grid=(n_tiles + 1, MOE_NJ),
        in_specs=[
            idx_spec(-1), idx_spec(0), idx_spec(1),
            pl.BlockSpec(memory_space=pl.ANY),
            pl.BlockSpec((None, None, D_MODEL, MOE_TF),
                         lambda t, j: (layer, tile(t) // nc, 0, jnp.minimum(j, MOE_NF - 1))),
            pl.BlockSpec((None, None, D_MODEL, MOE_TF),
                         lambda t, j: (layer, tile(t) // nc, 0, jnp.minimum(j, MOE_NF - 1))),
            pl.BlockSpec((None, None, EXPERT_FF, MOE_TN),
                         lambda t, j: (layer, tile(t) // nc, 0, jnp.clip(j - MOE_NF, 0, MOE_NN - 1))),
            pl.BlockSpec((ct, 1), lambda t, j: (tile(t), 0)),
            pl.BlockSpec(memory_space=pl.ANY),
        ],
        out_specs=pl.BlockSpec(memory_space=pl.ANY),
        out_shape=jax.ShapeDtypeStruct((N, D_MODEL), F32),
        input_output_aliases={8: 0},
        scratch_shapes=[
            pltpu.VMEM((2, ct, HALF_D), jnp.uint32),
            pltpu.VMEM((ct, D_MODEL), BF16),
            pltpu.VMEM((MOE_NF, ct, MOE_TF), BF16),
            pltpu.VMEM((2, MOE_NN, ct, MOE_TN), F32),
            pltpu.VMEM((2, ct // MOE_CHUNKS, D_MODEL), F32),
            pltpu.SemaphoreType.DMA((2,)),
            pltpu.SemaphoreType.DMA((2,)),
            pltpu.SemaphoreType.DMA((2,)),
        ],
        compiler_params=_params("arbitrary", "arbitrary"),
        name="moe",
    )(idx3, idx3, idx3, h2, w_gate, w_up, w_down, gval.reshape(-1, 1), x)


def _layer(x, b, L, layer, W, tabs):
    N = b * L
    ax_cos, ax_sin, pr_tabs = tabs
    h, h4, h16 = _attn_norm(x, W["norm_attn"][layer][None], b, L)
    qkv_a = _proj_a(h, W["w_in"], layer, W["gain_a"][layer], ax_cos, ax_sin, L)
    hds = (h.reshape(b, 1, L, D_MODEL), h4, h16)
    qkvb = [_proj_b(hds[g], W["w_in"], layer, g, W["gain_b"][layer], *pr_tabs[g])
            for g in range(B_GROUPS)]
    qkv_c = _proj_c(h, W["w_in"], layer, W["gain_c"][layer])
    oa = _attn_a(qkv_a.reshape(b, L, QKV_A)).reshape(N, A_WIDTH)
    ob = _attn_b(qkvb, b, L).reshape(N, B_WIDTH)
    oc = _attn_c(qkv_c.reshape(b, L, QKV_C), W["c_bias"][layer]).reshape(N, C_WIDTH)
    merged = _merge(h, oa, ob, oc, W["w_in"], W["wb_a"], W["wb_b"], W["wb_c"], layer)
    x, h2, aff_t = _out_proj(merged, x, W["w_out"], W["norm_ffn"][layer][None], W["w_rt"], layer)
    cap = EC_CAPACITY * N // N_EXPERTS
    gval, idx = lax.top_k(aff_t, cap)
    return _moe(x, h2, idx, gval, W["w_gate"], W["w_up"], W["w_down"], layer)


def _trunk(x3, W, depth):
    b, L, _ = x3.shape
    assert L % B_TT == 0 and L // B_DILATIONS[-1] >= 2 * B_HALF
    ax_cos, ax_sin = _axial_tables(L)
    pr_cos, pr_sin = _partial_tables(L)
    pr_tabs = [(_class_major(pr_cos, d), _class_major(pr_sin, d)) for d in B_DILATIONS]
    x = x3.reshape(b * L, D_MODEL)
    for layer in range(depth):
        x = _layer(x, b, L, layer, W, (ax_cos, ax_sin, pr_tabs))
    return x.reshape(b, L, D_MODEL)


def _prepare(norm_attn, w_in, qk_gain, rpb, w_branch, w_out, norm_ffn, w_router, w_gate, w_up,
             w_down):
    depth = w_in.shape[0]
    ones = jnp.ones((depth, HEAD_DIM), F32)
    rep = lambda v, k: jnp.tile(v, (1, k))
    wb = w_in.astype(BF16)
    seg = lambda lo, width: wb[..., lo:lo + width]
    qa0, ka0, va0 = 0, A_WIDTH, A_WIDTH + A_KV_HEADS * HEAD_DIM
    qb0 = QKV_A
    kb0 = qb0 + B_HEADS * HEAD_DIM
    vb0 = kb0 + B_HEADS * HEAD_DIM
    cols = [_perm_axial(seg(qa0, A_WIDTH)), _perm_axial(seg(ka0, A_KV_HEADS * HEAD_DIM)),
            seg(va0, A_KV_HEADS * HEAD_DIM)]
    for g in range(B_GROUPS):
        cols += [_perm_partial(seg(kb0 + g * B_WIDTH, B_WIDTH)), seg(vb0 + g * B_WIDTH, B_WIDTH),
                 _perm_partial(seg(qb0 + g * B_WIDTH, B_WIDTH))]
    cols.append(wb[..., QKV_A + B_GROUPS * QKV_B:])
    gq_a, gk_a = _perm_axial(qk_gain[:, 0, 0]), _perm_axial(qk_gain[:, 0, 1])
    gq_b, gk_b = _perm_partial(qk_gain[:, 1, 0]), _perm_partial(qk_gain[:, 1, 1])
    hp = B_HEADS_PER_GROUP
    return {
        "norm_attn": norm_attn, "norm_ffn": norm_ffn,
        "gain_a": jnp.concatenate([rep(gq_a, A_Q_HEADS), rep(gk_a, A_KV_HEADS),
                                   rep(ones, A_KV_HEADS)], axis=1)[:, None, :],
        "gain_b": jnp.concatenate([rep(gk_b, hp), rep(ones, hp), rep(gq_b, hp)], axis=1)[:, None, :],
        "gain_c": jnp.concatenate([rep(qk_gain[:, 2, 0], C_HEADS), rep(qk_gain[:, 2, 1], C_HEADS),
                                   rep(ones, C_HEADS)], axis=1)[:, None, :],
        "w_in": jnp.concatenate(cols, axis=-1),
        "wb_a": w_branch[:, :A_WIDTH].astype(BF16),
        "wb_b": w_branch[:, A_WIDTH:A_WIDTH + B_WIDTH].astype(BF16),
        "wb_c": w_branch[:, A_WIDTH + B_WIDTH:].astype(BF16),
        "w_out": w_out.astype(BF16),
        "w_rt": jnp.swapaxes(w_router, 1, 2).astype(BF16),
        "c_bias": jax.vmap(_c_bias_table)(rpb),
        "w_gate": w_gate.astype(BF16), "w_up": w_up.astype(BF16), "w_down": w_down.astype(BF16),
    }


def kernel(x_prompt, x_sample, norm_attn, w_in, qk_gain, rpb, w_branch, w_out, norm_ffn, w_router,
           w_gate, w_up, w_down):
    W = _prepare(norm_attn, w_in, qk_gain, rpb, w_branch, w_out, norm_ffn, w_router, w_gate,
                 w_up, w_down)
    depth = w_in.shape[0]
    return _trunk(x_prompt, W, depth), _trunk(x_sample, W, depth)
```

```python
import functools

import jax
import jax.numpy as jnp
from jax import lax
from jax.experimental import pallas as pl
from jax.experimental.pallas import tpu as pltpu

F32 = jnp.float32
BF16 = jnp.bfloat16

D_MODEL = 2048
HEAD_DIM = 128
GRID_W = 64
A_Q_HEADS = 8
A_KV_HEADS = 2
A_GROUP = A_Q_HEADS // A_KV_HEADS
AXIAL_THETA = 10000.0
B_DILATIONS = (1, 4, 16)
B_GROUPS = 3
B_HEADS_PER_GROUP = 4
B_HEADS = B_GROUPS * B_HEADS_PER_GROUP
B_HALF = 64
ROPE_THETA = 500000.0
ROPE_DIMS = HEAD_DIM // 4
C_HEADS = 8
C_WIN_ROWS = 8
C_WIN_COLS = 16
A_WIDTH = A_Q_HEADS * HEAD_DIM
B_WIDTH = B_HEADS_PER_GROUP * HEAD_DIM
C_WIDTH = C_HEADS * HEAD_DIM
QKV_A = (A_Q_HEADS + 2 * A_KV_HEADS) * HEAD_DIM
QKV_B = 3 * B_WIDTH
QKV_C = 3 * C_WIDTH
QKV_WIDTH = QKV_A + B_GROUPS * QKV_B + QKV_C
N_EXPERTS = 16
EXPERT_FF = 2048
EC_CAPACITY = 2
EPS = 1e-6
NEG = -1e30
SCALE = HEAD_DIM ** -0.5
LOG2E = 1.4426950408889634

VMEM_LIMIT = 56 * 1024 * 1024

PLAIN, NORM, ROPE = 0, 1, 2


def _params(*sem):
    return pltpu.CompilerParams(dimension_semantics=sem, vmem_limit_bytes=VMEM_LIMIT)


def _rms_rows(xf, g):
    ms = jnp.mean(xf * xf, axis=-1, keepdims=True)
    return xf * lax.rsqrt(ms + EPS) * g


def _nt_dot(a, b):
    return lax.dot_general(a, b, (((1,), (1,)), ((), ())), preferred_element_type=F32)


def _perm_axial(w):
    lead = w.shape[:-1]
    return w.reshape(*lead, -1, 2, 2, HEAD_DIM // 4).swapaxes(-2, -3).reshape(*lead, -1)


def _perm_partial(w):
    lead = w.shape[:-1]
    h = w.reshape(*lead, -1, HEAD_DIM)
    m = ROPE_DIMS // 2
    out = jnp.concatenate(
        [h[..., :m], h[..., 2 * m:HEAD_DIM // 2 + m], h[..., m:2 * m], h[..., HEAD_DIM // 2 + m:]],
        axis=-1)
    return out.reshape(*lead, -1)


def _angles(pos, dims, theta):
    inv = theta ** (-(jnp.arange(0, dims, 2, dtype=F32) / dims))
    return pos.astype(F32)[:, None] * inv[None, :]


def _axial_tables(L):
    t = jnp.arange(L)
    ra = _angles(t // GRID_W, HEAD_DIM // 2, AXIAL_THETA)
    ca = _angles(t % GRID_W, HEAD_DIM // 2, AXIAL_THETA)
    cos = jnp.concatenate([jnp.cos(ra), jnp.cos(ca), jnp.cos(ra), jnp.cos(ca)], axis=-1)
    sin = jnp.concatenate([-jnp.sin(ra), -jnp.sin(ca), jnp.sin(ra), jnp.sin(ca)], axis=-1)
    return cos, sin


def _partial_tables(L):
    t = jnp.arange(L)
    ang = _angles(t, ROPE_DIMS, ROPE_THETA)
    pad = HEAD_DIM // 2 - ROPE_DIMS // 2
    one, zero = jnp.ones((L, pad), F32), jnp.zeros((L, pad), F32)
    cos = jnp.concatenate([jnp.cos(ang), one, jnp.cos(ang), one], axis=-1)
    sin = jnp.concatenate([-jnp.sin(ang), zero, jnp.sin(ang), zero], axis=-1)
    return cos, sin


def _class_major(tab, dil):
    L = tab.shape[0]
    return tab.reshape(L // dil, dil, HEAD_DIM).transpose(1, 0, 2)


def _norm_kernel(x_ref, g_ref, h_ref, h4_ref, h16_ref, hf_ref, *, tm):
    xf = x_ref[...]
    inv = lax.rsqrt(jnp.mean(xf * xf, axis=-1, keepdims=True) + EPS)
    for j in range(D_MODEL // HEAD_DIM):
        cols = slice(j * HEAD_DIM, (j + 1) * HEAD_DIM)
        y = x_ref[:, cols] * inv * g_ref[:, cols]
        hf_ref[j] = y
        h_ref[:, cols] = y.astype(BF16)
    for dil, ref in ((B_DILATIONS[1], h4_ref), (B_DILATIONS[2], h16_ref)):
        for r in range(dil):
            for j in range(D_MODEL // HEAD_DIM):
                ref[r, :, j * HEAD_DIM:(j + 1) * HEAD_DIM] = hf_ref[
                    j, pl.ds(r, tm // dil, stride=dil), :].astype(BF16)


def _attn_norm(x, g, b, L):
    N = x.shape[0]
    tm = 512
    nl = L // tm
    d1, d2 = B_DILATIONS[1], B_DILATIONS[2]
    row = pl.BlockSpec((tm, D_MODEL), lambda i: (i, 0))
    return pl.pallas_call(
        functools.partial(_norm_kernel, tm=tm),
        grid=(N // tm,),
        in_specs=[row, pl.BlockSpec((1, D_MODEL), lambda i: (0, 0))],
        out_specs=(
            row,
            pl.BlockSpec((None, d1, tm // d1, D_MODEL), lambda i: (i // nl, 0, i % nl, 0)),
            pl.BlockSpec((None, d2, tm // d2, D_MODEL), lambda i: (i // nl, 0, i % nl, 0)),
        ),
        out_shape=(
            jax.ShapeDtypeStruct((N, D_MODEL), BF16),
            jax.ShapeDtypeStruct((b, d1, L // d1, D_MODEL), BF16),
            jax.ShapeDtypeStruct((b, d2, L // d2, D_MODEL), BF16),
        ),
        scratch_shapes=[pltpu.VMEM((D_MODEL // HEAD_DIM, tm, HEAD_DIM), F32)],
        compiler_params=_params("parallel"),
        name="attn_norm",
    )(x, g)


PROJ_TN = 512
PROJ_RC = 256


def _proj_kernel(*refs, tm, kinds, rope):
    if rope:
        h_ref, w_ref, gain_ref, cos_ref, sin_ref, o_ref = refs
    else:
        h_ref, w_ref, gain_ref, o_ref = refs
    heads_per_tile = PROJ_TN // HEAD_DIM
    rc = min(PROJ_RC, tm)
    for c in range(tm // rc):
        rows = slice(c * rc, (c + 1) * rc)
        hc = h_ref[rows, :]
        for t in range(len(kinds) // heads_per_tile):
            acc = jnp.dot(hc, w_ref[:, t * PROJ_TN:(t + 1) * PROJ_TN], preferred_element_type=F32)
            for hd in range(heads_per_tile):
                kind = kinds[t * heads_per_tile + hd]
                y = acc[:, hd * HEAD_DIM:(hd + 1) * HEAD_DIM]
                cols = slice(t * PROJ_TN + hd * HEAD_DIM, t * PROJ_TN + (hd + 1) * HEAD_DIM)
                if kind != PLAIN:
                    y = _rms_rows(y, gain_ref[:, cols])
                if kind == ROPE:
                    y = y * cos_ref[rows, :] + pltpu.roll(y, HEAD_DIM // 2, 1) * sin_ref[rows, :]
                o_ref[rows, cols] = y.astype(BF16)


_KINDS_A = (ROPE,) * (A_Q_HEADS + A_KV_HEADS) + (PLAIN,) * A_KV_HEADS
_KINDS_B = (ROPE,) * 4 + (PLAIN,) * 4 + (ROPE,) * 4
_KINDS_C = (NORM,) * (2 * C_HEADS) + (PLAIN,) * C_HEADS


def _proj_a(h, w_in, layer, gain, cos, sin, L):
    N = h.shape[0]
    tm = min(1024, L)
    nl = L // tm
    tab = pl.BlockSpec((tm, HEAD_DIM), lambda i: (i % nl, 0))
    return pl.pallas_call(
        functools.partial(_proj_kernel, tm=tm, kinds=_KINDS_A, rope=True),
        grid=(N // tm,),
        in_specs=[
            pl.BlockSpec((tm, D_MODEL), lambda i: (i, 0)),
            pl.BlockSpec((None, D_MODEL, QKV_A), lambda i: (layer, 0, 0)),
            pl.BlockSpec((1, QKV_A), lambda i: (0, 0)),
            tab, tab,
        ],
        out_specs=pl.BlockSpec((tm, QKV_A), lambda i: (i, 0)),
        out_shape=jax.ShapeDtypeStruct((N, QKV_A), BF16),
        compiler_params=_params("parallel"),
        name="proj_a",
    )(h, w_in, gain, cos, sin)


def _proj_b(hd, w_in, layer, g, gain, cos, sin):
    b, dil, n, _ = hd.shape
    tm = min(1024, n)
    col0 = QKV_A // QKV_B
    tab = pl.BlockSpec((None, tm, HEAD_DIM), lambda bi, r, i: (r, i, 0))
    return pl.pallas_call(
        functools.partial(_proj_kernel, tm=tm, kinds=_KINDS_B, rope=True),
        grid=(b, dil, n // tm),
        in_specs=[
            pl.BlockSpec((None, None, tm, D_MODEL), lambda bi, r, i: (bi, r, i, 0)),
            pl.BlockSpec((None, D_MODEL, QKV_B), lambda bi, r, i: (layer, 0, col0 + g)),
            pl.BlockSpec((1, QKV_B), lambda bi, r, i: (0, 0)),
            tab, tab,
        ],
        out_specs=pl.BlockSpec((None, None, tm, QKV_B), lambda bi, r, i: (bi, r, i, 0)),
        out_shape=jax.ShapeDtypeStruct((b, dil, n, QKV_B), BF16),
        compiler_params=_params("parallel", "parallel", "parallel"),
        name=f"proj_b{g}",
    )(hd, w_in, gain, cos, sin)


def _proj_c(h, w_in, layer, gain):
    N = h.shape[0]
    tm = 512
    col0 = (QKV_A + B_GROUPS * QKV_B) // QKV_C
    return pl.pallas_call(
        functools.partial(_proj_kernel, tm=tm, kinds=_KINDS_C, rope=False),
        grid=(N // tm,),
        in_specs=[
            pl.BlockSpec((tm, D_MODEL), lambda i: (i, 0)),
            pl.BlockSpec((None, D_MODEL, QKV_C), lambda i: (layer, 0, col0)),
            pl.BlockSpec((1, QKV_C), lambda i: (0, 0)),
        ],
        out_specs=pl.BlockSpec((tm, QKV_C), lambda i: (i, 0)),
        out_shape=jax.ShapeDtypeStruct((N, QKV_C), BF16),
        compiler_params=_params("parallel"),
        name="proj_c",
    )(h, w_in, gain)


A_TQ = 256
A_TK = 1024
A_UNROLL = 8


def _attn_a_kernel(q_ref, k_ref, v_ref, o_ref, qs_ref, m_ref, l_ref, acc_ref, *, L):
    tq, tk = A_TQ, A_TK
    c2 = SCALE * LOG2E
    for h in range(A_GROUP):
        qs_ref[h * tq:(h + 1) * tq, :] = q_ref[:, h * HEAD_DIM:(h + 1) * HEAD_DIM]
    m_ref[...] = jnp.full(m_ref.shape, -jnp.inf, F32)
    l_ref[...] = jnp.zeros(l_ref.shape, F32)
    acc_ref[...] = jnp.zeros(acc_ref.shape, F32)

    def body(c, carry):
        c0 = pl.multiple_of(c * tk, tk)
        s = _nt_dot(qs_ref[...], k_ref[pl.ds(c0, tk), :])
        m_prev = m_ref[...]
        m_next = jnp.maximum(m_prev, jnp.max(s, axis=1, keepdims=True))
        p = jnp.exp2((s - jnp.concatenate([m_next] * (tk // HEAD_DIM), axis=1)) * c2)
        alpha = jnp.exp2((m_prev - m_next) * c2)
        l_ref[...] = alpha * l_ref[...] + jnp.sum(p, axis=1, keepdims=True)
        acc_ref[...] = acc_ref[...] * alpha + jnp.dot(
            p.astype(BF16), v_ref[pl.ds(c0, tk), :], preferred_element_type=F32)
        m_ref[...] = m_next
        return carry

    lax.fori_loop(0, L // tk, body, 0, unroll=min(A_UNROLL, L // tk))
    o = acc_ref[...] / l_ref[...]
    for h in range(A_GROUP):
        o_ref[:, h * HEAD_DIM:(h + 1) * HEAD_DIM] = o[h * tq:(h + 1) * tq].astype(BF16)


def _attn_a(qkv3):
    b, L, _ = qkv3.shape
    rows = A_GROUP * A_TQ
    qw = A_GROUP * HEAD_DIM
    return pl.pallas_call(
        functools.partial(_attn_a_kernel, L=L),
        grid=(b, A_KV_HEADS, L // A_TQ),
        in_specs=[
            pl.BlockSpec((None, A_TQ, qw), lambda bi, kv, qi: (bi, qi, kv)),
            pl.BlockSpec((None, L, HEAD_DIM), lambda bi, kv, qi: (bi, 0, A_Q_HEADS + kv)),
            pl.BlockSpec((None, L, HEAD_DIM),
                         lambda bi, kv, qi: (bi, 0, A_Q_HEADS + A_KV_HEADS + kv)),
        ],
        out_specs=pl.BlockSpec((None, A_TQ, qw), lambda bi, kv, qi: (bi, qi, kv)),
        out_shape=jax.ShapeDtypeStruct((b, L, A_WIDTH), BF16),
        scratch_shapes=[
            pltpu.VMEM((rows, HEAD_DIM), BF16),
            pltpu.VMEM((rows, HEAD_DIM), F32),
            pltpu.VMEM((rows, HEAD_DIM), F32),
            pltpu.VMEM((rows, HEAD_DIM), F32),
        ],
        compiler_params=_params("parallel", "parallel", "arbitrary"),
        name="attn_a",
    )(qkv3, qkv3, qkv3)


B_TT = 1024
B_SUB = 128


def _band_tiles(qs, ks, vs, bases, n):
    nq, span = qs[0].shape[0], ks[0].shape[0]
    row = lax.broadcasted_iota(jnp.int32, (nq, span), 0)
    col = lax.broadcasted_iota(jnp.int32, (nq, span), 1)
    band = (col >= row) & (col - row <= 2 * B_HALF)
    ss = [_nt_dot(q, k) * SCALE for q, k in zip(qs, ks)]
    ss = [jnp.where(band & (col + base >= 0) & (col + base < n), s, NEG)
          for s, base in zip(ss, bases)]
    ms = [jnp.max(s, axis=1, keepdims=True) for s in ss]
    ps = [jnp.exp(s - m) for s, m in zip(ss, ms)]
    ls = [jnp.sum(p, axis=1, keepdims=True) for p in ps]
    pns = [(p * (1.0 / l)).astype(BF16) for p, l in zip(ps, ls)]
    os = [jnp.dot(pn, v, preferred_element_type=F32) for pn, v in zip(pns, vs)]
    return os, [m + jnp.log(l) for m, l in zip(ms, ls)]


def _attn_b_kernel(kv0, q0, p0, n0, kv1, q1, p1, n1, kv2, q2, p2, n2, o_ref, os_ref, ls_ref,
                   *, L):
    i = pl.program_id(1)
    d1, d2 = B_DILATIONS[1], B_DILATIONS[2]
    kcol = lambda h: slice(h * HEAD_DIM, (h + 1) * HEAD_DIM)
    vcol = lambda h: slice(B_WIDTH + h * HEAD_DIM, B_WIDTH + (h + 1) * HEAD_DIM)

    pc2 = B_TT // d2

    heads = range(B_HEADS_PER_GROUP)

    def halo_cat(prev, cur, nxt, r, cols):
        return jnp.concatenate([prev[r, :, cols], cur[r, :, cols], nxt[r, :, cols]], axis=0)

    def class2(it, carry):
        rs = [2 * it, 2 * it + 1]
        tiles = [(r, h) for r in rs for h in heads]
        qs = [q2[r, :, kcol(h)] for r, h in tiles]
        ks = [halo_cat(p2, kv2, n2, r, kcol(h)) for r, h in tiles]
        vs = [halo_cat(p2, kv2, n2, r, vcol(h)) for r, h in tiles]
        os, lses = _band_tiles(qs, ks, vs, [i * pc2 - B_HALF] * len(tiles), L // d2)
        for (r, h), o, lse in zip(tiles, os, lses):
            os_ref[1, h, pl.ds(r, pc2, stride=d2), :] = o
            ls_ref[1, h, pl.ds(r, pc2, stride=d2), :] = jnp.broadcast_to(lse, (pc2, HEAD_DIM))
        return carry

    lax.fori_loop(0, d2 // 2, class2, 0)

    pc1 = B_TT // d1
    span = B_SUB + 2 * B_HALF

    def class1(r, carry):
        tiles = [(h, jq * B_SUB) for h in heads for jq in range(pc1 // B_SUB)]
        kf = [halo_cat(p1, kv1, n1, r, kcol(h)) for h in heads]
        vf = [halo_cat(p1, kv1, n1, r, vcol(h)) for h in heads]
        qs = [q1[r, lo:lo + B_SUB, kcol(h)] for h, lo in tiles]
        ks = [kf[h][lo:lo + span] for h, lo in tiles]
        vs = [vf[h][lo:lo + span] for h, lo in tiles]
        os, lses = _band_tiles(qs, ks, vs, [i * pc1 + lo - B_HALF for _, lo in tiles], L // d1)
        for (h, lo), o, lse in zip(tiles, os, lses):
            rows = pl.ds(r + d1 * lo, B_SUB, stride=d1)
            os_ref[0, h, rows, :] = o
            ls_ref[0, h, rows, :] = jnp.broadcast_to(lse, (B_SUB, HEAD_DIM))
        return carry

    lax.fori_loop(0, d1, class1, 0)

    nsub = B_TT // B_SUB
    for jq in range(nsub):
        lo = jq * B_SUB
        rows = slice(lo, lo + B_SUB)

        def window(ref_p, ref_c, ref_n, cols):
            if jq == 0:
                return jnp.concatenate([ref_p[0, :, cols], ref_c[0, :B_SUB + B_HALF, cols]], axis=0)
            if jq == nsub - 1:
                return jnp.concatenate([ref_c[0, lo - B_HALF:, cols], ref_n[0, :, cols]], axis=0)
            return ref_c[0, lo - B_HALF:lo + B_SUB + B_HALF, cols]

        o0s, lse0s = _band_tiles([q0[0, rows, kcol(h)] for h in heads],
                                 [window(p0, kv0, n0, kcol(h)) for h in heads],
                                 [window(p0, kv0, n0, vcol(h)) for h in heads],
                                 [i * B_TT + lo - B_HALF] * len(heads), L)
        for h in heads:
            o0 = o0s[h]
            a0 = jnp.broadcast_to(lse0s[h], (B_SUB, HEAD_DIM))
            a1 = ls_ref[0, h, rows, :]
            a2 = ls_ref[1, h, rows, :]
            m = jnp.maximum(jnp.maximum(a0, a1), a2)
            e0, e1, e2 = jnp.exp(a0 - m), jnp.exp(a1 - m), jnp.exp(a2 - m)
            inv = 1.0 / (e0 + e1 + e2)
            out = (e0 * inv) * o0 + (e1 * inv) * os_ref[0, h, rows, :] \
                + (e2 * inv) * os_ref[1, h, rows, :]
            o_ref[rows, kcol(h)] = out.astype(BF16)


def _attn_b(qkvb, b, L):
    in_specs, args = [], []
    for g, dil in enumerate(B_DILATIONS):
        n = L // dil
        pc = B_TT // dil
        hb = pc // B_HALF
        nhb = n // B_HALF
        in_specs += [
            pl.BlockSpec((None, dil, pc, 2 * B_WIDTH), lambda bi, i: (bi, 0, i, 0)),
            pl.BlockSpec((None, dil, pc, B_WIDTH), lambda bi, i: (bi, 0, i, 2)),
            pl.BlockSpec((None, dil, B_HALF, 2 * B_WIDTH),
                         lambda bi, i, hb=hb: (bi, 0, jnp.maximum(i * hb - 1, 0), 0)),
            pl.BlockSpec((None, dil, B_HALF, 2 * B_WIDTH),
                         lambda bi, i, hb=hb, nhb=nhb: (bi, 0, jnp.minimum((i + 1) * hb, nhb - 1), 0)),
        ]
        args += [qkvb[g]] * 4
    return pl.pallas_call(
        functools.partial(_attn_b_kernel, L=L),
        grid=(b, L // B_TT),
        in_specs=in_specs,
        out_specs=pl.BlockSpec((None, B_TT, B_WIDTH), lambda bi, i: (bi, i, 0)),
        out_shape=jax.ShapeDtypeStruct((b, L, B_WIDTH), BF16),
        scratch_shapes=[pltpu.VMEM((2, B_HEADS_PER_GROUP, B_TT, HEAD_DIM), F32)] * 2,
        compiler_params=_params("parallel", "arbitrary"),
        name="attn_b",
    )(*args)


C_ROWS_PER_ITER = 16


def _c_bias_table(rpb):
    c = jnp.arange(GRID_W)
    cs = jnp.clip(c - C_WIN_COLS // 2, 0, GRID_W - C_WIN_COLS)
    col_ok = (c[None, :] >= cs[:, None]) & (c[None, :] < cs[:, None] + C_WIN_COLS)
    dc = jnp.clip(c[None, :] - c[:, None], -(C_WIN_COLS - 1), C_WIN_COLS - 1) + C_WIN_COLS - 1
    cb = rpb[:, :, dc].astype(F32)
    cb = jnp.where(col_ok[None, None], cb, NEG)
    a = jnp.arange(C_WIN_ROWS)[:, None] + jnp.arange(C_WIN_ROWS)[None, :]
    bt = cb[:, a]
    bt = bt.transpose(0, 1, 3, 2, 4)
    return bt.reshape(C_HEADS, C_WIN_ROWS, GRID_W, C_WIN_ROWS * GRID_W)


def _attn_c_kernel(q_ref, k_ref, v_ref, bt_ref, o_ref, *, rows_per_step, R):
    qi = pl.program_id(2)
    kw = C_WIN_ROWS * GRID_W

    def body(it, carry):
        q0s, k0s, deltas = [], [], []
        for u in range(C_ROWS_PER_ITER):
            rr = it * C_ROWS_PER_ITER + u
            r = qi * rows_per_step + rr
            rs = jnp.clip(r - C_WIN_ROWS // 2, 0, R - C_WIN_ROWS)
            deltas.append(rs - r + (C_WIN_ROWS - 1))
            q0s.append(pl.multiple_of(rr * GRID_W, GRID_W))
            k0s.append(pl.multiple_of(rs * GRID_W, GRID_W))
        qs = [q_ref[pl.ds(q0, GRID_W), :] for q0 in q0s]
        ks = [k_ref[pl.ds(k0, kw), :] for k0 in k0s]
        vs = [v_ref[pl.ds(k0, kw), :] for k0 in k0s]
        bs = [bt_ref[d] for d in deltas]
        ss = [_nt_dot(q, k) * SCALE + bias for q, k, bias in zip(qs, ks, bs)]
        ms = [jnp.max(s, axis=1, keepdims=True) for s in ss]
        ps = [jnp.exp(s - m) for s, m in zip(ss, ms)]
        ls = [jnp.sum(p, axis=1, keepdims=True) for p in ps]
        pns = [(p * (1.0 / l)).astype(BF16) for p, l in zip(ps, ls)]
        outs = [jnp.dot(pn, v, preferred_element_type=F32).astype(BF16) for pn, v in zip(pns, vs)]
        for q0, o in zip(q0s, outs):
            o_ref[pl.ds(q0, GRID_W), :] = o
        return carry

    lax.fori_loop(0, rows_per_step // C_ROWS_PER_ITER, body, 0)


def _attn_c(qkv3, bt):
    b, L, _ = qkv3.shape
    R = L // GRID_W
    assert R >= C_WIN_ROWS
    tq = min(2048, L)
    return pl.pallas_call(
        functools.partial(_attn_c_kernel, rows_per_step=tq // GRID_W, R=R),
        grid=(b, C_HEADS, L // tq),
        in_specs=[
            pl.BlockSpec((None, tq, HEAD_DIM), lambda bi, h, qi: (bi, qi, h)),
            pl.BlockSpec((None, L, HEAD_DIM), lambda bi, h, qi: (bi, 0, C_HEADS + h)),
            pl.BlockSpec((None, L, HEAD_DIM), lambda bi, h, qi: (bi, 0, 2 * C_HEADS + h)),
            pl.BlockSpec((None, C_WIN_ROWS, GRID_W, C_WIN_ROWS * GRID_W),
                         lambda bi, h, qi: (h, 0, 0, 0)),
        ],
        out_specs=pl.BlockSpec((None, tq, HEAD_DIM), lambda bi, h, qi: (bi, qi, h)),
        out_shape=jax.ShapeDtypeStruct((b, L, C_WIDTH), BF16),
        compiler_params=_params("parallel", "parallel", "arbitrary"),
        name="attn_c",
    )(qkv3, qkv3, qkv3, bt)


MERGE_TN = 512


def _merge_kernel(h_ref, oa_ref, ob_ref, oc_ref, wga_ref, wgb_ref, wgc_ref,
                  wba_ref, wbb_ref, wbc_ref, o_ref):
    h = h_ref[...]

    def branch(wg_ref, o_in_ref, wb_ref):
        gate = jax.nn.sigmoid(jnp.dot(h, wg_ref[...], preferred_element_type=F32))
        return gate * jnp.dot(o_in_ref[...], wb_ref[...], preferred_element_type=F32)

    merged = branch(wga_ref, oa_ref, wba_ref)
    merged = merged + branch(wgb_ref, ob_ref, wbb_ref)
    merged = merged + branch(wgc_ref, oc_ref, wbc_ref)
    o_ref[...] = merged.astype(BF16)


def _merge(h, oa, ob, oc, w_in, wba, wbb, wbc, layer):
    N = h.shape[0]
    tm = 512
    tn = MERGE_TN
    gcol = QKV_WIDTH // tn
    dcol = D_MODEL // tn

    def rows(width):
        return pl.BlockSpec((tm, width), lambda i, j: (i, 0))

    def gate_w(k):
        return pl.BlockSpec((None, D_MODEL, tn), lambda i, j: (layer, 0, gcol + k * dcol + j))

    def branch_w(width):
        return pl.BlockSpec((None, width, tn), lambda i, j: (layer, 0, j))

    return pl.pallas_call(
        _merge_kernel,
        grid=(N // tm, dcol),
        in_specs=[
            rows(D_MODEL), rows(A_WIDTH), rows(B_WIDTH), rows(C_WIDTH),
            gate_w(0), gate_w(1), gate_w(2),
            branch_w(A_WIDTH), branch_w(B_WIDTH), branch_w(C_WIDTH),
        ],
        out_specs=pl.BlockSpec((tm, tn), lambda i, j: (i, j)),
        out_shape=jax.ShapeDtypeStruct((N, D_MODEL), BF16),
        compiler_params=_params("parallel", "arbitrary"),
        name="merge",
    )(h, oa, ob, oc, w_in, w_in, w_in, wba, wbb, wbc)


HALF_D = D_MODEL // 2
HI16 = 0xFFFF0000


def _pack_bf16_pairs(hb):
    bits = lax.bitcast_convert_type(hb.astype(F32), jnp.uint32)
    return (bits[:, HALF_D:] & jnp.uint32(HI16)) | (bits[:, :HALF_D] >> jnp.uint32(16))


def _unpack_bf16_pairs(pk):
    lo = lax.bitcast_convert_type(pk << jnp.uint32(16), F32)
    hi = lax.bitcast_convert_type(pk & jnp.uint32(HI16), F32)
    return lo.astype(BF16), hi.astype(BF16)


def _out_kernel(m_ref, x_ref, w_ref, g_ref, wr_ref, xo_ref, h_ref, aff_ref):
    xn = x_ref[...] + jnp.dot(m_ref[...], w_ref[...], preferred_element_type=F32)
    xo_ref[...] = xn
    h = _rms_rows(xn, g_ref[...]).astype(BF16)
    h_ref[...] = _pack_bf16_pairs(h)
    logits = _nt_dot(wr_ref[...], h)
    mx = jnp.max(logits, axis=0, keepdims=True)
    e = jnp.exp(logits - mx)
    aff_ref[...] = e / jnp.sum(e, axis=0, keepdims=True)


def _out_proj(merged, x, w_out, g, w_rt, layer):
    N = x.shape[0]
    tm = 512
    row = pl.BlockSpec((tm, D_MODEL), lambda i: (i, 0))
    return pl.pallas_call(
        _out_kernel,
        grid=(N // tm,),
        in_specs=[
            row, row,
            pl.BlockSpec((None, D_MODEL, D_MODEL), lambda i: (layer, 0, 0)),
            pl.BlockSpec((1, D_MODEL), lambda i: (0, 0)),
            pl.BlockSpec((None, N_EXPERTS, D_MODEL), lambda i: (layer, 0, 0)),
        ],
        out_specs=(row, pl.BlockSpec((tm, HALF_D), lambda i: (i, 0)),
                   pl.BlockSpec((N_EXPERTS, tm), lambda i: (0, i))),
        out_shape=(
            jax.ShapeDtypeStruct((N, D_MODEL), F32),
            jax.ShapeDtypeStruct((N, HALF_D), jnp.uint32),
            jax.ShapeDtypeStruct((N_EXPERTS, N), F32),
        ),
        compiler_params=_params("parallel"),
        name="out_proj",
    )(merged, x, w_out, g, w_rt)


MOE_CT = 1024
MOE_TF = 256
MOE_TN = 512
MOE_NF = EXPERT_FF // MOE_TF
MOE_NN = D_MODEL // MOE_TN
MOE_NJ = MOE_NF + MOE_NN
MOE_CHUNKS = 4
MOE_RC = 256


def _moe_kernel(idx_prev, idx_cur, idx_next, h2_hbm, wg_ref, wu_ref, wd_ref, gv_ref, x_in_hbm,
                x_hbm, xe_ref, xeb_ref, hid_ref, ye_ref, xb_ref, gsem, xsem,
                *, ct, n_tiles):
    del x_in_hbm
    t = pl.program_id(0)
    j = pl.program_id(1)
    slot = t % 2
    pslot = 1 - slot
    rb = ct // MOE_CHUNKS
    rc = min(MOE_RC, ct)
    g_big = 3 * ct // 8

    def xe_row_copy(idx_ref, dst_slot, r):
        return pltpu.make_async_copy(h2_hbm.at[pl.ds(idx_ref[0, r], 1)],
                                     xe_ref.at[dst_slot, pl.ds(r, 1)], gsem.at[dst_slot])

    def xe_all_rows(dst_slot):
        return pltpu.make_async_copy(h2_hbm.at[pl.ds(0, ct)], xe_ref.at[dst_slot],
                                     gsem.at[dst_slot])

    def x_row_in(k, buf, r):
        return pltpu.make_async_copy(x_hbm.at[pl.ds(idx_prev[0, k * rb + r], 1)],
                                     xb_ref.at[buf, pl.ds(r, 1)], xsem.at[buf])

    def x_row_out(k, buf, r):
        return pltpu.make_async_copy(xb_ref.at[buf, pl.ds(r, 1)],
                                     x_hbm.at[pl.ds(idx_prev[0, k * rb + r], 1)],
                                     xsem.at[2 + buf])

    def wait_x_chunk(sem_index):
        pltpu.make_async_copy(x_hbm.at[pl.ds(0, rb)], xb_ref.at[sem_index % 2],
                              xsem.at[sem_index]).wait()

    def end_wait(kind):
        if kind == "x":
            wait_x_chunk((j - 1) % 4)
        elif kind == "next_rows":
            xe_all_rows(pslot).wait()

    def scatter_add_copies(kind):
        buf = j % 2
        if kind == "gather":
            return [x_row_in(j, buf, r) for r in range(rb)]
        if kind == "regather":
            return [x_row_in(j - 2, buf, r) for r in range(rb)]
        if kind == "add":
            k = jnp.where(j < 4, j - 2, j - 4)
            r0 = pl.multiple_of(k * rb, rb)
            for n in range(MOE_NN):
                cols = slice(n * MOE_TN, (n + 1) * MOE_TN)
                xb_ref[buf, :, cols] = xb_ref[buf, :, cols] + ye_ref[pslot, n, pl.ds(r0, rb), :]
            return [x_row_out(k, buf, r) for r in range(rb)]
        return []

    def scatter_add_only(kind, wait_kind):
        for cp in scatter_add_copies(kind):
            cp.start()
        end_wait(wait_kind)

    row_chunks = [slice(c * rc, (c + 1) * rc) for c in range(ct // rc)]

    def start_group(copies, g, n_groups):
        per = -(-len(copies) // n_groups)
        for cp in copies[g * per:(g + 1) * per]:
            cp.start()

    @pl.when((t == 0) & (j == 0))
    def _():
        ye_ref[1] = jnp.zeros(ye_ref.shape[1:], F32)

        def body(r, carry):
            xe_row_copy(idx_cur, 0, r).start()
            return carry
        lax.fori_loop(0, ct, body, 0)
        xe_all_rows(0).wait()

    def gate_up_step(kind, wait_kind, first):
        if first:
            lo, hi = _unpack_bf16_pairs(xe_ref[slot])
            xeb_ref[:, :HALF_D] = lo
            xeb_ref[:, HALF_D:] = hi
        copies = scatter_add_copies(kind)
        groups = 2 * len(row_chunks)
        for c, rows in enumerate(row_chunks):
            xe = xeb_ref[rows, :]
            gate = jnp.dot(xe, wg_ref[...], preferred_element_type=F32)
            start_group(copies, 2 * c, groups)
            up = jnp.dot(xe, wu_ref[...], preferred_element_type=F32)
            hid_ref[j, rows, :] = (jax.nn.silu(gate) * up).astype(BF16)
            start_group(copies, 2 * c + 1, groups)
        end_wait(wait_kind)

    def down_step(wait_kind, g0, n_gather):
        copies = [xe_row_copy(idx_next, pslot, g0 + r) for r in range(n_gather)]
        groups = MOE_NF * len(row_chunks)
        for c, rows in enumerate(row_chunks):
            acc = jnp.dot(hid_ref[0, rows, :], wd_ref[:MOE_TF, :], preferred_element_type=F32)
            start_group(copies, MOE_NF * c, groups)
            for f in range(1, MOE_NF):
                acc = acc + jnp.dot(hid_ref[f, rows, :], wd_ref[f * MOE_TF:(f + 1) * MOE_TF, :],
                                    preferred_element_type=F32)
                start_group(copies, MOE_NF * c + f, groups)
            ye_ref[slot, j - MOE_NF, rows, :] = acc * gv_ref[rows, :]
        end_wait(wait_kind)

    real = t < n_tiles
    steps = (
        (j == 0, "gather", None, functools.partial(gate_up_step, "gather", None, True)),
        (j == 1, "gather", "x", functools.partial(gate_up_step, "gather", "x", False)),
        ((j == 2) | (j == 3) | (j == 6) | (j == 7), "add", "x",
         functools.partial(gate_up_step, "add", "x", False)),
        ((j == 4) | (j == 5), "regather", "x", functools.partial(gate_up_step, "regather", "x", False)),
        (j == 8, None, "x", functools.partial(down_step, "x", 0, g_big)),
        (j == 9, None, None, functools.partial(down_step, None, g_big, g_big)),
        (j == 10, None, None, functools.partial(down_step, None, 2 * g_big, ct - 2 * g_big)),
        (j == 11, None, None, functools.partial(down_step, "next_rows", 0, 0)),
    )
    for cond, kind, wait_kind, fn in steps:
        pl.when(real & cond)(fn)
        if kind is not None or wait_kind is not None:
            pl.when(jnp.logical_not(real) & cond)(
                functools.partial(scatter_add_only, kind, wait_kind))


def _moe(x, h2, idx, gval, w_gate, w_up, w_down, layer):
    N = x.shape[0]
    E, cap = idx.shape
    ct = min(MOE_CT, cap)
    nc = cap // ct
    n_tiles = E * nc
    idx3 = idx.reshape(n_tiles, 1, ct)
    last = n_tiles - 1
    tile = lambda t: jnp.minimum(t, last)

    def idx_spec(off):
        return pl.BlockSpec((None, 1, ct), lambda t, j: (jnp.clip(t + off, 0, last), 0, 0),
                            memory_space=pltpu.SMEM)

    return pl.pallas_call(
        functools.partial(_moe_kernel, ct=ct, n_tiles=n_tiles),
        grid=(n_tiles + 1, MOE_NJ),
        in_specs=[
            idx_spec(-1), idx_spec(0), idx_spec(1),
            pl.BlockSpec(memory_space=pl.ANY),
            pl.BlockSpec((None, None, D_MODEL, MOE_TF),
                         lambda t, j: (layer, tile(t) // nc, 0, jnp.minimum(j, MOE_NF - 1))),
            pl.BlockSpec((None, None, D_MODEL, MOE_TF),
                         lambda t, j: (layer, tile(t) // nc, 0, jnp.minimum(j, MOE_NF - 1))),
            pl.BlockSpec((None, None, EXPERT_FF, MOE_TN),
                         lambda t, j: (layer, tile(t) // nc, 0, jnp.clip(j - MOE_NF, 0, MOE_NN - 1))),
            pl.BlockSpec((ct, 1), lambda t, j: (tile(t), 0)),
            pl.BlockSpec(memory_space=pl.ANY),
        ],
        out_specs=pl.BlockSpec(memory_space=pl.ANY),
        out_shape=jax.ShapeDtypeStruct((N, D_MODEL), F32),
        input_output_aliases={8: 0},
        scratch_shapes=[
            pltpu.VMEM((2, ct, HALF_D), jnp.uint32),
            pltpu.VMEM((ct, D_MODEL), BF16),
            pltpu.VMEM((MOE_NF, ct, MOE_TF), BF16),
            pltpu.VMEM((2, MOE_NN, ct, MOE_TN), F32),
            pltpu.VMEM((2, ct // MOE_CHUNKS, D_MODEL), F32),
            pltpu.SemaphoreType.DMA((2,)),
            pltpu.SemaphoreType.DMA((4,)),
        ],
        compiler_params=_params("arbitrary", "arbitrary"),
        name="moe",
    )(idx3, idx3, idx3, h2, w_gate, w_up, w_down, gval.reshape(-1, 1), x)


def _layer(x, b, L, layer, W, tabs):
    N = b * L
    ax_cos, ax_sin, pr_tabs = tabs
    h, h4, h16 = _attn_norm(x, W["norm_attn"][layer][None], b, L)
    qkv_a = _proj_a(h, W["w_in"], layer, W["gain_a"][layer], ax_cos, ax_sin, L)
    hds = (h.reshape(b, 1, L, D_MODEL), h4, h16)
    qkvb = [_proj_b(hds[g], W["w_in"], layer, g, W["gain_b"][layer], *pr_tabs[g])
            for g in range(B_GROUPS)]
    qkv_c = _proj_c(h, W["w_in"], layer, W["gain_c"][layer])
    oa = _attn_a(qkv_a.reshape(b, L, QKV_A)).reshape(N, A_WIDTH)
    ob = _attn_b(qkvb, b, L).reshape(N, B_WIDTH)
    oc = _attn_c(qkv_c.reshape(b, L, QKV_C), W["c_bias"][layer]).reshape(N, C_WIDTH)
    merged = _merge(h, oa, ob, oc, W["w_in"], W["wb_a"], W["wb_b"], W["wb_c"], layer)
    x, h2, aff_t = _out_proj(merged, x, W["w_out"], W["norm_ffn"][layer][None], W["w_rt"], layer)
    cap = EC_CAPACITY * N // N_EXPERTS
    gval, idx = lax.top_k(aff_t, cap)
    return _moe(x, h2, idx, gval, W["w_gate"], W["w_up"], W["w_down"], layer)


def _trunk(x3, W, depth):
    b, L, _ = x3.shape
    assert L % B_TT == 0 and L // B_DILATIONS[-1] >= 2 * B_HALF
    ax_cos, ax_sin = _axial_tables(L)
    pr_cos, pr_sin = _partial_tables(L)
    pr_tabs = [(_class_major(pr_cos, d), _class_major(pr_sin, d)) for d in B_DILATIONS]
    x = x3.reshape(b * L, D_MODEL)
    for layer in range(depth):
        x = _layer(x, b, L, layer, W, (ax_cos, ax_sin, pr_tabs))
    return x.reshape(b, L, D_MODEL)


def _prepare(norm_attn, w_in, qk_gain, rpb, w_branch, w_out, norm_ffn, w_router, w_gate, w_up,
             w_down):
    depth = w_in.shape[0]
    ones = jnp.ones((depth, HEAD_DIM), F32)
    rep = lambda v, k: jnp.tile(v, (1, k))
    wb = w_in.astype(BF16)
    seg = lambda lo, width: wb[..., lo:lo + width]
    qa0, ka0, va0 = 0, A_WIDTH, A_WIDTH + A_KV_HEADS * HEAD_DIM
    qb0 = QKV_A
    kb0 = qb0 + B_HEADS * HEAD_DIM
    vb0 = kb0 + B_HEADS * HEAD_DIM
    cols = [_perm_axial(seg(qa0, A_WIDTH)), _perm_axial(seg(ka0, A_KV_HEADS * HEAD_DIM)),
            seg(va0, A_KV_HEADS * HEAD_DIM)]
    for g in range(B_GROUPS):
        cols += [_perm_partial(seg(kb0 + g * B_WIDTH, B_WIDTH)), seg(vb0 + g * B_WIDTH, B_WIDTH),
                 _perm_partial(seg(qb0 + g * B_WIDTH, B_WIDTH))]
    cols.append(wb[..., QKV_A + B_GROUPS * QKV_B:])
    gq_a, gk_a = _perm_axial(qk_gain[:, 0, 0]), _perm_axial(qk_gain[:, 0, 1])
    gq_b, gk_b = _perm_partial(qk_gain[:, 1, 0]), _perm_partial(qk_gain[:, 1, 1])
    hp = B_HEADS_PER_GROUP
    return {
        "norm_attn": norm_attn, "norm_ffn": norm_ffn,
        "gain_a": jnp.concatenate([rep(gq_a, A_Q_HEADS), rep(gk_a, A_KV_HEADS),
                                   rep(ones, A_KV_HEADS)], axis=1)[:, None, :],
        "gain_b": jnp.concatenate([rep(gk_b, hp), rep(ones, hp), rep(gq_b, hp)], axis=1)[:, None, :],
        "gain_c": jnp.concatenate([rep(qk_gain[:, 2, 0], C_HEADS), rep(qk_gain[:, 2, 1], C_HEADS),
                                   rep(ones, C_HEADS)], axis=1)[:, None, :],
        "w_in": jnp.concatenate(cols, axis=-1),
        "wb_a": w_branch[:, :A_WIDTH].astype(BF16),
        "wb_b": w_branch[:, A_WIDTH:A_WIDTH + B_WIDTH].astype(BF16),
        "wb_c": w_branch[:, A_WIDTH + B_WIDTH:].astype(BF16),
        "w_out": w_out.astype(BF16),
        "w_rt": jnp.swapaxes(w_router, 1, 2).astype(BF16),
        "c_bias": jax.vmap(_c_bias_table)(rpb),
        "w_gate": w_gate.astype(BF16), "w_up": w_up.astype(BF16), "w_down": w_down.astype(BF16),
    }


def kernel(x_prompt, x_sample, norm_attn, w_in, qk_gain, rpb, w_branch, w_out, norm_ffn, w_router,
           w_gate, w_up, w_down):
    W = _prepare(norm_attn, w_in, qk_gain, rpb, w_branch, w_out, norm_ffn, w_router, w_gate,
                 w_up, w_down)
    depth = w_in.shape[0]
    return _trunk(x_prompt, W, depth), _trunk(x_sample, W, depth)
```
